```python
import jax, jax.numpy as jnp
from jax import lax
import numpy as np

D_MODEL = 1024
BATCH = 1
SEQ = 16384
DEPTH = 1

RG_WIDTH = 1024
RG_BLOCKS = 4
RG_BLOCK_DIM = RG_WIDTH // RG_BLOCKS
RG_CONV = 4
RG_C = 8.0
GDN_QK_HEADS = 8
GDN_V_HEADS = 16
GDN_DK = 128
GDN_DV = 128
GDN_KEY_DIM = GDN_QK_HEADS * GDN_DK
GDN_VAL_DIM = GDN_V_HEADS * GDN_DV
GDN_CONV = 4
GDN_CHUNK = 64
N_EXPERTS = 32
TOP_K = 4
D_EXPERT = 1024
SWIGLU_ALPHA = 1.702
SWIGLU_LIMIT = 7.0
MOE_BLOCK = 128
EPS = 1e-6

IN_SIZES = (RG_WIDTH, RG_WIDTH,
            GDN_KEY_DIM, GDN_KEY_DIM, GDN_VAL_DIM,
            GDN_VAL_DIM,
            GDN_V_HEADS, GDN_V_HEADS,
            D_MODEL, D_MODEL)
D_IN = 2 * RG_WIDTH + 2 * GDN_KEY_DIM + 2 * GDN_VAL_DIM + 2 * GDN_V_HEADS + 2 * D_MODEL

kernel_name = "hybrid_rglru_gdn_moe_block"


def rms_norm(x, w):
    xf = x.astype(jnp.float32)
    y = xf * lax.rsqrt(jnp.mean(xf * xf, axis=-1, keepdims=True) + EPS)
    return (y * w.astype(jnp.float32)).astype(x.dtype)


def split_cols(z, sizes):
    outs, off = [], 0
    for s in sizes:
        outs.append(z[..., off:off + s])
        off += s
    return outs


def causal_depthwise_conv(x, w):
    k_w = w.shape[0]
    c = x.shape[-1]
    return lax.conv_general_dilated(x, w[:, None, :].astype(x.dtype), window_strides=(1,),
                                    padding=[(k_w - 1, 0)],
                                    dimension_numbers=('NWC', 'WIO', 'NWC'),
                                    feature_group_count=c)


def block_diag_linear(x, w, b):
    xh = x.reshape(x.shape[:-1] + (RG_BLOCKS, RG_BLOCK_DIM))
    y = jnp.einsum('bthi,hij->bthj', xh, w) + b
    return y.reshape(x.shape)


def rg_lru(x, w_a, b_a, w_x, b_x, lam):
    t_len = x.shape[1]
    xf = x.astype(jnp.float32)
    r = jax.nn.sigmoid(block_diag_linear(x, w_a, b_a).astype(jnp.float32))
    i = jax.nn.sigmoid(block_diag_linear(x, w_x, b_x).astype(jnp.float32))
    log_a = -RG_C * r * jax.nn.softplus(-lam.astype(jnp.float32))
    a = jnp.exp(log_a)
    mult = jnp.sqrt(jnp.maximum(-jnp.expm1(2.0 * log_a), 0.0))
    first = (jnp.arange(t_len) == 0)[None, :, None]
    mult = jnp.where(first, 1.0, mult)
    b = mult * (i * xf)

    def combine(c1, c2):
        a1, b1 = c1
        a2, b2 = c2
        return a1 * a2, a2 * b1 + b2

    _, h = lax.associative_scan(combine, (a, b), axis=1)
    return h.astype(x.dtype)


def l2norm(x):
    xf = x.astype(jnp.float32)
    return xf * lax.rsqrt(jnp.sum(xf * xf, axis=-1, keepdims=True) + EPS)


def gated_delta_rule_chunked(q, k, v, g, beta):
    b_sz, t_len, n_h, d_k = q.shape
    d_v = v.shape[-1]
    c = GDN_CHUNK
    n_c = t_len // c
    f32 = jnp.float32
    q = q.astype(f32) * (d_k ** -0.5)
    k = k.astype(f32)
    v = v.astype(f32)
    g = g.astype(f32)
    beta = beta.astype(f32)

    def to_chunks(t):
        return jnp.moveaxis(t.reshape((b_sz, n_c, c) + t.shape[2:]), 3, 1)

    qc, kc, vc, bc = to_chunks(q), to_chunks(k), to_chunks(v), to_chunks(beta)
    gc = jnp.cumsum(to_chunks(g), axis=-1)
    idx = jnp.arange(c)
    causal = idx[:, None] >= idx[None, :]
    strict = idx[:, None] > idx[None, :]
    diff = gc[..., :, None] - gc[..., None, :]
    decay = jnp.where(causal, jnp.exp(jnp.where(causal, diff, 0.0)), 0.0)

    kk = jnp.einsum('bhncd,bhnsd->bhncs', kc, kc)
    a_mat = jnp.where(strict, bc[..., None] * kk * decay, 0.0)
    eye = jnp.eye(c, dtype=f32)
    t_mat = lax.linalg.triangular_solve(eye + a_mat, jnp.broadcast_to(eye, a_mat.shape),
                                        left_side=True, lower=True, unit_diagonal=True)
    u = jnp.einsum('bhncs,bhnsd->bhncd', t_mat, vc * bc[..., None])
    w = jnp.einsum('bhncs,bhnsd->bhncd', t_mat, kc * (bc * jnp.exp(gc))[..., None])
    attn_intra = jnp.einsum('bhncd,bhnsd->bhncs', qc, kc) * decay
    q_dec = qc * jnp.exp(gc)[..., None]
    g_last = gc[..., -1]
    k_dec = kc * jnp.exp(g_last[..., None] - gc)[..., None]

    def step(s, inp):
        u_n, w_n, a_n, qd_n, kd_n, gl_n = inp
        v_new = u_n - jnp.einsum('bhcd,bhde->bhce', w_n, s)
        o = jnp.einsum('bhcd,bhde->bhce', qd_n, s) + jnp.einsum('bhcs,bhse->bhce', a_n, v_new)
        s = s * jnp.exp(gl_n)[..., None, None] + jnp.einsum('bhcd,bhce->bhde', kd_n, v_new)
        return s, o

    xs = tuple(jnp.moveaxis(t, 2, 0) for t in (u, w, attn_intra, q_dec, k_dec, g_last))
    s0 = jnp.zeros((b_sz, n_h, d_k, d_v), f32)
    _, o = lax.scan(step, s0, xs)
    return jnp.transpose(o, (1, 0, 3, 2, 4)).reshape(b_sz, t_len, n_h, d_v)


def moe_ffn(xn, router_w, router_b, w_gu, b_gu, w_down, b_down):
    b_sz, t_len, d = xn.shape
    n_tok = b_sz * t_len
    xt = xn.reshape(n_tok, d)
    logits = (xt @ router_w + router_b).astype(jnp.float32)
    top_vals, top_idx = lax.top_k(logits, TOP_K)
    gates = jax.nn.softmax(top_vals, axis=-1)

    n_as = n_tok * TOP_K
    flat_e = top_idx.reshape(n_as).astype(jnp.int32)
    flat_tok = (jnp.arange(n_as, dtype=jnp.int32) // TOP_K)
    flat_w = gates.reshape(n_as)
    order = jnp.argsort(flat_e, stable=True)
    e_sorted = flat_e[order]
    counts = jnp.zeros((N_EXPERTS,), jnp.int32).at[flat_e].add(1)
    starts = jnp.cumsum(counts) - counts
    padded = ((counts + MOE_BLOCK - 1) // MOE_BLOCK) * MOE_BLOCK
    pends = jnp.cumsum(padded)
    pstarts = pends - padded
    rank = jnp.arange(n_as, dtype=jnp.int32) - starts[e_sorted]
    dest = pstarts[e_sorted] + rank
    n_blocks = (n_as + MOE_BLOCK - 1) // MOE_BLOCK + N_EXPERTS
    n_rows = n_blocks * MOE_BLOCK
    row_tok = jnp.zeros((n_rows,), jnp.int32).at[dest].set(flat_tok[order])
    row_w = jnp.zeros((n_rows,), jnp.float32).at[dest].set(flat_w[order])
    block_start = jnp.arange(n_blocks, dtype=jnp.int32) * MOE_BLOCK
    block_e = jnp.minimum(jnp.searchsorted(pends, block_start, side='right'), N_EXPERTS - 1)

    x_rows = xt[row_tok].reshape(n_blocks, MOE_BLOCK, d)

    def expert_block(args):
        xb, e = args
        gu = xb @ w_gu[e] + b_gu[e]
        gate, up = gu[:, :D_EXPERT], gu[:, D_EXPERT:]
        gate = jnp.minimum(gate, SWIGLU_LIMIT)
        up = jnp.clip(up, -SWIGLU_LIMIT, SWIGLU_LIMIT)
        hid = (up + 1.0) * (gate * jax.nn.sigmoid(SWIGLU_ALPHA * gate))
        return hid @ w_down[e] + b_down[e]

    y_rows = lax.map(expert_block, (x_rows, block_e)).reshape(n_rows, d)
    y = jnp.zeros((n_tok, d), jnp.float32).at[row_tok].add(y_rows.astype(jnp.float32) * row_w[:, None])
    return y.reshape(b_sz, t_len, d).astype(xn.dtype)


def setup_inputs(seed: int = 0) -> dict:
    key = jax.random.key(seed)
    ks = jax.random.split(key, 24)
    f32 = jnp.float32
    nrm = lambda k, shape, s: jax.random.normal(k, shape, f32) * s
    gain = lambda k, shape: 1.0 + 0.02 * jax.random.normal(k, shape, f32)
    u = jax.random.uniform(ks[8], (DEPTH, RG_WIDTH), f32, 0.9, 0.999)
    s = u ** (1.0 / RG_C)
    return {
        "x": nrm(ks[0], (BATCH, SEQ, D_MODEL), 1.0),
        "norm_mix_w": gain(ks[1], (DEPTH, D_MODEL)),
        "w_in": nrm(ks[2], (DEPTH, D_MODEL, D_IN), D_MODEL ** -0.5),
        "rg_conv_w": nrm(ks[3], (DEPTH, RG_CONV, RG_WIDTH), RG_CONV ** -0.5),
        "rg_conv_b": nrm(ks[4], (DEPTH, RG_WIDTH), 0.01),
        "rg_gate_a_w": nrm(ks[5], (DEPTH, RG_BLOCKS, RG_BLOCK_DIM, RG_BLOCK_DIM), RG_BLOCK_DIM ** -0.5),
        "rg_gate_a_b": nrm(ks[6], (DEPTH, RG_BLOCKS, RG_BLOCK_DIM), 0.01),
        "rg_gate_x_w": nrm(ks[7], (DEPTH, RG_BLOCKS, RG_BLOCK_DIM, RG_BLOCK_DIM), RG_BLOCK_DIM ** -0.5),
        "rg_gate_x_b": nrm(ks[9], (DEPTH, RG_BLOCKS, RG_BLOCK_DIM), 0.01),
        "rg_lambda": jnp.log(s) - jnp.log1p(-s),
        "gdn_conv_w": nrm(ks[10], (DEPTH, GDN_CONV, 2 * GDN_KEY_DIM + GDN_VAL_DIM), GDN_CONV ** -0.5),
        "gdn_A_log": jnp.log(jax.random.uniform(ks[11], (DEPTH, GDN_V_HEADS), f32, 1.0, 16.0)),
        "gdn_dt_bias": nrm(ks[12], (DEPTH, GDN_V_HEADS), 0.1),
        "gdn_norm_w": gain(ks[13], (DEPTH, GDN_DV)),
        "w_branch_rg": nrm(ks[14], (DEPTH, RG_WIDTH, D_MODEL), RG_WIDTH ** -0.5),
        "w_branch_gdn": nrm(ks[15], (DEPTH, GDN_VAL_DIM, D_MODEL), GDN_VAL_DIM ** -0.5),
        "w_out": nrm(ks[16], (DEPTH, D_MODEL, D_MODEL), D_MODEL ** -0.5),
        "norm_ffn_w": gain(ks[17], (DEPTH, D_MODEL)),
        "router_w": nrm(ks[18], (DEPTH, D_MODEL, N_EXPERTS), D_MODEL ** -0.5),
        "router_b": nrm(ks[19], (DEPTH, N_EXPERTS), 0.01),
        "moe_w_gate_up": nrm(ks[20], (DEPTH, N_EXPERTS, D_MODEL, 2 * D_EXPERT), D_MODEL ** -0.5),
        "moe_b_gate_up": nrm(ks[21], (DEPTH, N_EXPERTS, 2 * D_EXPERT), 0.01),
        "moe_w_down": nrm(ks[22], (DEPTH, N_EXPERTS, D_EXPERT, D_MODEL), D_EXPERT ** -0.5),
        "moe_b_down": nrm(ks[23], (DEPTH, N_EXPERTS, D_MODEL), 0.01),
        "norm_final_w": gain(jax.random.fold_in(key, 99), (D_MODEL,)),
    }


def reference(x, norm_mix_w, w_in, rg_conv_w, rg_conv_b, rg_gate_a_w, rg_gate_a_b, rg_gate_x_w,
              rg_gate_x_b, rg_lambda, gdn_conv_w, gdn_A_log, gdn_dt_bias, gdn_norm_w, w_branch_rg,
              w_branch_gdn, w_out, norm_ffn_w, router_w, router_b, moe_w_gate_up, moe_b_gate_up,
              moe_w_down, moe_b_down, norm_final_w):
    b_sz, t_len, _ = x.shape
    rep = GDN_V_HEADS // GDN_QK_HEADS
    h = x
    for l in range(DEPTH):
        xn = rms_norm(h, norm_mix_w[l])
        z = xn @ w_in[l]
        rg_x, rg_y, q_raw, k_raw, v_raw, z_gate, b_raw, a_raw, gate_a, gate_b = split_cols(z, IN_SIZES)

        xa = causal_depthwise_conv(rg_x, rg_conv_w[l]) + rg_conv_b[l]
        ha = rg_lru(xa, rg_gate_a_w[l], rg_gate_a_b[l], rg_gate_x_w[l], rg_gate_x_b[l], rg_lambda[l])
        ya = (ha * jax.nn.gelu(rg_y)) @ w_branch_rg[l]

        qkv = jax.nn.silu(causal_depthwise_conv(jnp.concatenate([q_raw, k_raw, v_raw], axis=-1), gdn_conv_w[l]))
        q, k, v = split_cols(qkv, (GDN_KEY_DIM, GDN_KEY_DIM, GDN_VAL_DIM))
        q = jnp.repeat(l2norm(q.reshape(b_sz, t_len, GDN_QK_HEADS, GDN_DK)), rep, axis=2)
        k = jnp.repeat(l2norm(k.reshape(b_sz, t_len, GDN_QK_HEADS, GDN_DK)), rep, axis=2)
        v = v.reshape(b_sz, t_len, GDN_V_HEADS, GDN_DV)
        beta = jax.nn.sigmoid(b_raw.astype(jnp.float32))
        g = -jnp.exp(gdn_A_log[l].astype(jnp.float32)) * jax.nn.softplus(
            a_raw.astype(jnp.float32) + gdn_dt_bias[l].astype(jnp.float32))
        o = gated_delta_rule_chunked(q, k, v, g, beta).astype(x.dtype)
        o = rms_norm(o, gdn_norm_w[l]) * jax.nn.silu(z_gate.reshape(b_sz, t_len, GDN_V_HEADS, GDN_DV))
        yb = o.reshape(b_sz, t_len, GDN_VAL_DIM) @ w_branch_gdn[l]

        mix = jax.nn.sigmoid(gate_a) * ya + jax.nn.sigmoid(gate_b) * yb
        h = h + mix @ w_out[l]

        hn = rms_norm(h, norm_ffn_w[l])
        h = h + moe_ffn(hn, router_w[l], router_b[l], moe_w_gate_up[l], moe_b_gate_up[l],
                        moe_w_down[l], moe_b_down[l])
    return rms_norm(h, norm_final_w)
```

```python
import functools

import jax
import jax.numpy as jnp
from jax import lax
from jax.experimental import pallas as pl
from jax.experimental.pallas import tpu as pltpu

F32 = jnp.float32
BF16 = jnp.bfloat16
I32 = jnp.int32
HI = lax.Precision.HIGHEST

D_MODEL = 1024
RG_WIDTH = 1024
RG_BLOCKS = 4
RG_BLOCK_DIM = RG_WIDTH // RG_BLOCKS
RG_C = 8.0
CONV_K = 4
QK_HEADS = 8
V_HEADS = 16
HEAD_DIM = 128
KEY_DIM = QK_HEADS * HEAD_DIM
VAL_DIM = V_HEADS * HEAD_DIM
CHUNK = 64
N_EXPERTS = 32
TOP_K = 4
D_EXPERT = 1024
SWIGLU_ALPHA = 1.702
SWIGLU_LIMIT = 7.0
EPS = 1e-6

LANES = 128
SUBLANES = 8
VMEM_LIMIT = 56 * 1024 * 1024

Z_RGX, Z_RGY, Z_Q, Z_K = 0, 1, 2, 3
Z_V, Z_ZG = 2, 3
Z_GA, Z_GB = 8, 9
Z_WIDTH = 10 * 1024

TM_INPROJ = 1024
TN_INPROJ = 1024
TM_RG = 256
TM_PREP = 256
GDN_HG = 4
GDN_NCH = 4
TM_MERGE = 512
TM_ROWS = 256
MOE_TILE = 512


def _cparams(sem):
    return pltpu.CompilerParams(dimension_semantics=sem, vmem_limit_bytes=VMEM_LIMIT)


def _softplus(x):
    return jnp.maximum(x, 0.0) + jnp.log1p(jnp.exp(-jnp.abs(x)))


def _sigmoid(x):
    return 1.0 / (1.0 + jnp.exp(-x))


def _silu(x):
    return x * _sigmoid(x)


def _gelu_tanh(x):
    c = 0.7978845608028654
    return 0.5 * x * (1.0 + jnp.tanh(c * (x + 0.044715 * (x * x * x))))


def _rms(x, w):
    return x * lax.rsqrt(jnp.mean(x * x, axis=-1, keepdims=True) + EPS) * w


def _dot(a, b):
    return jnp.dot(a, b, preferred_element_type=F32)


def _dot_hi(a, b):
    return jnp.dot(a, b, precision=HI, preferred_element_type=F32)


def _dot_nt(a, b, precision=None):
    return lax.dot_general(a, b, (((1,), (1,)), ((), ())), precision=precision,
                           preferred_element_type=F32)


def _dot_tn(a, b):
    return lax.dot_general(a, b, (((0,), (0,)), ((), ())), preferred_element_type=F32)


def _inproj_kernel(x_ref, nw_ref, w_ref, wba_ref, wbat_ref, z_ref, ba_ref, bat_ref, xn_sc):
    @pl.when(pl.program_id(1) == 0)
    def _():
        xn = _rms(x_ref[...], nw_ref[...])
        xn_sc[...] = xn.astype(BF16)
        ba_ref[...] = _dot_hi(xn, wba_ref[...])
        bat_ref[...] = _dot_nt(wbat_ref[...], xn, precision=HI)

    z_ref[...] = _dot(xn_sc[...], w_ref[...]).astype(BF16)


def _inproj(x, nw, w_z, w_ba, w_bat):
    t = x.shape[0]
    tm, tn = min(TM_INPROJ, t), TN_INPROJ
    return pl.pallas_call(
        _inproj_kernel,
        grid=(t // tm, Z_WIDTH // tn),
        in_specs=[
            pl.BlockSpec((tm, D_MODEL), lambda i, j: (i, 0)),
            pl.BlockSpec((1, D_MODEL), lambda i, j: (0, 0)),
            pl.BlockSpec((D_MODEL, tn), lambda i, j: (0, j)),
            pl.BlockSpec((D_MODEL, LANES), lambda i, j: (0, 0)),
            pl.BlockSpec((LANES, D_MODEL), lambda i, j: (0, 0)),
        ],
        out_specs=[
            pl.BlockSpec((tm, tn), lambda i, j: (i, j)),
            pl.BlockSpec((tm, LANES), lambda i, j: (i, 0)),
            pl.BlockSpec((LANES, tm), lambda i, j: (0, i)),
        ],
        out_shape=[
            jax.ShapeDtypeStruct((t, Z_WIDTH), BF16),
            jax.ShapeDtypeStruct((t, LANES), F32),
            jax.ShapeDtypeStruct((LANES, t), F32),
        ],
        scratch_shapes=[pltpu.VMEM((tm, D_MODEL), BF16)],
        compiler_params=_cparams(("arbitrary", "arbitrary")),
        name="inproj",
    )(x, nw, w_z, w_ba, w_bat)


def _causal_conv(x, prev, cw):
    tm = x.shape[0]
    xe = jnp.concatenate([prev, x], axis=0)
    off = SUBLANES - (CONV_K - 1)
    acc = cw[0:1, :] * xe[off:off + tm, :]
    for k in range(1, CONV_K):
        acc = acc + cw[k:k + 1, :] * xe[off + k:off + k + tm, :]
    return acc


def _rg_kernel(x_ref, y_ref, cw_ref, cb_ref, wa_ref, ba_ref, wx_ref, bx_ref, lam_ref, out_ref,
               prev_sc, h_sc):
    i = pl.program_id(0)
    tm = x_ref.shape[0]

    @pl.when(i == 0)
    def _():
        prev_sc[...] = jnp.zeros_like(prev_sc)
        h_sc[...] = jnp.zeros_like(h_sc)

    x = x_ref[...].astype(F32)
    xa = _causal_conv(x, prev_sc[...], cw_ref[...]) + cb_ref[...]
    prev_sc[...] = x[tm - SUBLANES:, :]

    xab = xa.astype(BF16)
    r_parts, i_parts = [], []
    for blk in range(RG_BLOCKS):
        xs = xab[:, blk * RG_BLOCK_DIM:(blk + 1) * RG_BLOCK_DIM]
        r_parts.append(_dot(xs, wa_ref[blk]))
        i_parts.append(_dot(xs, wx_ref[blk]))
    r = _sigmoid(jnp.concatenate(r_parts, axis=1) + ba_ref[...])
    ig = _sigmoid(jnp.concatenate(i_parts, axis=1) + bx_ref[...])

    log_a = (-RG_C) * r * _softplus(-lam_ref[...])
    a = jnp.exp(log_a)
    mult = jnp.sqrt(jnp.maximum(1.0 - a * a, 0.0))
    rows = lax.broadcasted_iota(I32, (tm, 1), 0)
    mult = jnp.where(jnp.logical_and(rows == 0, i == 0), 1.0, mult)
    b = mult * (ig * xa)

    s = 1
    while s < tm:
        keep = rows >= s
        a_sh = pltpu.roll(a, s, axis=0)
        b_sh = pltpu.roll(b, s, axis=0)
        b = b + jnp.where(keep, a * b_sh, 0.0)
        a = jnp.where(keep, a * a_sh, a)
        s *= 2
    h = b + a * h_sc[...]
    h_sc[...] = h[tm - 1:tm, :]
    out_ref[...] = (h * _gelu_tanh(y_ref[...].astype(F32))).astype(BF16)


def _rg_branch(z, cw, cb, wa, ba, wx, bx, lam):
    t = z.shape[0]
    tm = min(TM_RG, t)
    full = lambda shape: pl.BlockSpec(shape, lambda i: (0,) * len(shape))
    return pl.pallas_call(
        _rg_kernel,
        grid=(t // tm,),
        in_specs=[
            pl.BlockSpec((tm, RG_WIDTH), lambda i: (i, Z_RGX)),
            pl.BlockSpec((tm, RG_WIDTH), lambda i: (i, Z_RGY)),
            full((CONV_K, RG_WIDTH)),
            full((1, RG_WIDTH)),
            full((RG_BLOCKS, RG_BLOCK_DIM, RG_BLOCK_DIM)),
            full((1, RG_WIDTH)),
            full((RG_BLOCKS, RG_BLOCK_DIM, RG_BLOCK_DIM)),
            full((1, RG_WIDTH)),
            full((1, RG_WIDTH)),
        ],
        out_specs=pl.BlockSpec((tm, RG_WIDTH), lambda i: (i, 0)),
        out_shape=jax.ShapeDtypeStruct((t, RG_WIDTH), BF16),
        scratch_shapes=[pltpu.VMEM((SUBLANES, RG_WIDTH), F32), pltpu.VMEM((1, RG_WIDTH), F32)],
        compiler_params=_cparams(("arbitrary",)),
        name="rg_branch",
    )(z, z, cw, cb, wa, ba, wx, bx, lam)


def _l2norm_heads(x, n_heads):
    parts = []
    for h in range(n_heads):
        xh = x[:, h * HEAD_DIM:(h + 1) * HEAD_DIM]
        parts.append(xh * lax.rsqrt(jnp.sum(xh * xh, axis=-1, keepdims=True) + EPS))
    return jnp.concatenate(parts, axis=1)


def _gdnprep_kernel(q_ref, k_ref, v_ref, cw_ref, qo_ref, ko_ref, vo_ref, prev_sc):
    tm = q_ref.shape[0]

    @pl.when(pl.program_id(0) == 0)
    def _():
        prev_sc[...] = jnp.zeros_like(prev_sc)

    def conv_silu(ref, lo, width):
        x = ref[...].astype(F32)
        y = _causal_conv(x, prev_sc[:, lo:lo + width], cw_ref[:, lo:lo + width])
        prev_sc[:, lo:lo + width] = x[tm - SUBLANES:, :]
        return _silu(y)

    q = conv_silu(q_ref, 0, KEY_DIM)
    qo_ref[...] = _l2norm_heads(q, QK_HEADS).astype(BF16)
    k = conv_silu(k_ref, KEY_DIM, KEY_DIM)
    ko_ref[...] = _l2norm_heads(k, QK_HEADS).astype(BF16)
    v = conv_silu(v_ref, 2 * KEY_DIM, VAL_DIM)
    vo_ref[...] = v.astype(BF16)


def _gdn_prep(z, cw):
    t = z.shape[0]
    tm = min(TM_PREP, t)
    return pl.pallas_call(
        _gdnprep_kernel,
        grid=(t // tm,),
        in_specs=[
            pl.BlockSpec((tm, KEY_DIM), lambda i: (i, Z_Q)),
            pl.BlockSpec((tm, KEY_DIM), lambda i: (i, Z_K)),
            pl.BlockSpec((tm, VAL_DIM), lambda i: (i, Z_V)),
            pl.BlockSpec((CONV_K, 2 * KEY_DIM + VAL_DIM), lambda i: (0, 0)),
        ],
        out_specs=[
            pl.BlockSpec((tm, KEY_DIM), lambda i: (i, 0)),
            pl.BlockSpec((tm, KEY_DIM), lambda i: (i, 0)),
            pl.BlockSpec((tm, VAL_DIM), lambda i: (i, 0)),
        ],
        out_shape=[
            jax.ShapeDtypeStruct((t, KEY_DIM), BF16),
            jax.ShapeDtypeStruct((t, KEY_DIM), BF16),
            jax.ShapeDtypeStruct((t, VAL_DIM), BF16),
        ],
        scratch_shapes=[pltpu.VMEM((SUBLANES, 2 * KEY_DIM + VAL_DIM), F32)],
        compiler_params=_cparams(("arbitrary",)),
        name="gdn_prep",
    )(z, z, z, cw)


def _unit_lower_inverse(a):
    c = a.shape[0]
    ri = lax.broadcasted_iota(I32, (c, c), 0)
    ci = lax.broadcasted_iota(I32, (c, c), 1)
    p = jnp.where(ri == ci, 1.0, 0.0) - a
    ab = a.astype(BF16)
    x = _dot(ab, ab)
    power = 2
    while power < c:
        xb = x.astype(BF16)
        p = p + _dot(p.astype(BF16), xb)
        power *= 2
        if power < c:
            x = _dot(xb, xb)
    return p


def _gdn_kernel(q_ref, k_ref, v_ref, ba_ref, bat_ref, arow_ref, acol_ref, mtri_ref, mblk_ref,
                o_ref, s_sc, gt_sc, *, hg_n, nch):
    grp = pl.program_id(0)
    c = CHUNK
    tile = nch * c
    scale = HEAD_DIM ** -0.5

    @pl.when(pl.program_id(1) == 0)
    def _():
        s_sc[...] = jnp.zeros_like(s_sc)

    ba = ba_ref[...]
    arow = arow_ref[...]
    g_all = -jnp.exp(arow[0:1, :]) * _softplus(ba + arow[1:2, :])
    beta_all = _sigmoid(ba)
    mtri = mtri_ref[...]
    gcs = _dot_hi(mtri, g_all)
    gls = _dot_hi(mblk_ref[...], g_all)
    egc = jnp.exp(gcs)
    ekd = jnp.exp(gls - gcs)
    acol = acol_ref[...]
    g_t = -jnp.exp(acol[:, 0:1]) * _softplus(bat_ref[...] + acol[:, 1:2])
    gt_sc[...] = _dot_nt(g_t, mtri, precision=HI)

    lane = lax.broadcasted_iota(I32, (tile, LANES), 1)
    ri = lax.broadcasted_iota(I32, (c, c), 0)
    ci = lax.broadcasted_iota(I32, (c, c), 1)
    causal = ri >= ci
    strict = ri > ci

    def column(arr, lane_idx):
        return jnp.sum(jnp.where(lane == lane_idx, arr, 0.0), axis=1, keepdims=True)

    cols, states = [], []
    for j in range(hg_n):
        head = grp * hg_n + j
        b_col = column(beta_all, head)
        egc_col = column(egc, V_HEADS + head)
        cols.append(dict(
            b=b_col, gcs=column(gcs, V_HEADS + head), egc=egc_col,
            ekd=column(ekd, V_HEADS + head), be=b_col * egc_col,
            g_row=gt_sc[pl.ds(V_HEADS + head, 1), :]))
        states.append(s_sc[j])

    for n in range(nch):
        r0 = n * c
        rows = slice(r0, r0 + c)
        for j in range(hg_n):
            qh = j // 2
            col = cols[j]
            if j % 2 == 0:
                k = k_ref[rows, qh * HEAD_DIM:(qh + 1) * HEAD_DIM]
                q = q_ref[rows, qh * HEAD_DIM:(qh + 1) * HEAD_DIM]
                kk, qk = _dot_nt(k, k), _dot_nt(q, k)
                kf, qf = k.astype(F32), q.astype(F32)
            v = v_ref[rows, j * HEAD_DIM:(j + 1) * HEAD_DIM]
            bc = col["b"][rows, :]
            g_row = col["g_row"]
            diff = col["gcs"][rows, :] - g_row[:, rows]
            decay = jnp.where(causal, jnp.exp(jnp.where(causal, diff, 0.0)), 0.0)
            a_mat = jnp.where(strict, bc * kk * decay, 0.0)
            t_mat = _unit_lower_inverse(a_mat).astype(BF16)
            rhs = jnp.concatenate([(v.astype(F32) * bc).astype(BF16),
                                   (kf * col["be"][rows, :]).astype(BF16)], axis=1)
            uw = _dot(t_mat, rhs)
            u, w = uw[:, :HEAD_DIM], uw[:, HEAD_DIM:]
            q_dec = (qf * col["egc"][rows, :]).astype(BF16)
            s = states[j]
            ws_qs = _dot(jnp.concatenate([w.astype(BF16), q_dec], axis=0), s.astype(BF16))
            v_new = (u - ws_qs[:c, :]).astype(BF16)
            o = (ws_qs[c:, :] + _dot((qk * decay).astype(BF16), v_new)) * scale
            o_ref[rows, j * HEAD_DIM:(j + 1) * HEAD_DIM] = o.astype(BF16)
            k_dec = (kf * col["ekd"][rows, :]).astype(BF16)
            egl = jnp.exp(g_row[:, r0 + c - 1:r0 + c])
            states[j] = s * egl + _dot_tn(k_dec, v_new)
    for j in range(hg_n):
        s_sc[j] = states[j]


def _gdn_core(q, k, v, ba, bat, arow, acol):
    t = q.shape[0]
    hg_n, nch = GDN_HG, GDN_NCH
    tile = nch * CHUNK
    idx = jnp.arange(tile)
    same = (idx[:, None] // CHUNK) == (idx[None, :] // CHUNK)
    mblk = same.astype(F32)
    mtri = jnp.logical_and(same, idx[:, None] >= idx[None, :]).astype(F32)
    qw = (hg_n // 2) * HEAD_DIM
    vw = hg_n * HEAD_DIM
    return pl.pallas_call(
        functools.partial(_gdn_kernel, hg_n=hg_n, nch=nch),
        grid=(V_HEADS // hg_n, t // tile),
        in_specs=[
            pl.BlockSpec((tile, qw), lambda g, n: (n, g)),
            pl.BlockSpec((tile, qw), lambda g, n: (n, g)),
            pl.BlockSpec((tile, vw), lambda g, n: (n, g)),
            pl.BlockSpec((tile, LANES), lambda g, n: (n, 0)),
            pl.BlockSpec((LANES, tile), lambda g, n: (0, n)),
            pl.BlockSpec((2, LANES), lambda g, n: (0, 0)),
            pl.BlockSpec((LANES, 2), lambda g, n: (0, 0)),
            pl.BlockSpec((tile, tile), lambda g, n: (0, 0)),
            pl.BlockSpec((tile, tile), lambda g, n: (0, 0)),
        ],
        out_specs=pl.BlockSpec((tile, vw), lambda g, n: (n, g)),
        out_shape=jax.ShapeDtypeStruct((t, VAL_DIM), BF16),
        scratch_shapes=[pltpu.VMEM((hg_n, HEAD_DIM, HEAD_DIM), F32), pltpu.VMEM((LANES, tile), F32)],
        compiler_params=_cparams(("arbitrary", "arbitrary")),
        name="gdn_core",
    )(q, k, v, ba, bat, arow, acol, mtri, mblk)


def _merge_kernel(x_ref, hg_ref, o_ref, zg_ref, ga_ref, gb_ref, wrg_ref, wgdn_ref, wout_ref,
                  gnw_ref, fnw_ref, rw_ref, rb_ref, lstrict_ref,
                  h_ref, hn_ref, sel_ref, gate_ref, cnt_ref, carry_sc):
    tm = x_ref.shape[0]

    @pl.when(pl.program_id(0) == 0)
    def _():
        carry_sc[...] = jnp.zeros_like(carry_sc)

    ya = _dot(hg_ref[...], wrg_ref[...])
    o = o_ref[...].astype(F32)
    zg = zg_ref[...].astype(F32)
    gnw = gnw_ref[...]
    parts = []
    for h in range(V_HEADS):
        sl = slice(h * HEAD_DIM, (h + 1) * HEAD_DIM)
        parts.append((_rms(o[:, sl], gnw) * _silu(zg[:, sl])).astype(BF16))
    yb = _dot(jnp.concatenate(parts, axis=1), wgdn_ref[...])
    mix = _sigmoid(ga_ref[...].astype(F32)) * ya + _sigmoid(gb_ref[...].astype(F32)) * yb
    h = x_ref[...] + _dot(mix.astype(BF16), wout_ref[...])
    h_ref[...] = h

    hn = _rms(h, fnw_ref[...])
    hn_ref[...] = hn
    lane = lax.broadcasted_iota(I32, (tm, LANES), 1)
    lane_f = lane.astype(F32)
    neg_inf = float("-inf")
    logits = jnp.where(lane < N_EXPERTS, _dot_hi(hn, rw_ref[...]) + rb_ref[...], neg_inf)
    vals, idxs = [], []
    multi_hot = jnp.zeros((tm, LANES), F32)
    for _ in range(TOP_K):
        m = jnp.max(logits, axis=-1, keepdims=True)
        first = jnp.min(jnp.where(logits == m, lane_f, float(LANES)), axis=-1, keepdims=True)
        hit = lane_f == first
        multi_hot = multi_hot + jnp.where(hit, 1.0, 0.0)
        logits = jnp.where(hit, neg_inf, logits)
        vals.append(m)
        idxs.append(first)
    exps = [jnp.exp(v - vals[0]) for v in vals]
    denom = exps[0] + exps[1] + exps[2] + exps[3]

    before = _dot(lstrict_ref[...], multi_hot.astype(BF16)) + carry_sc[...]
    carry_sc[...] = carry_sc[...] + jnp.sum(multi_hot, axis=0, keepdims=True)
    cnt_ref[...] = carry_sc[...].astype(I32)

    sel = jnp.zeros((tm, LANES), F32)
    gate = jnp.zeros((tm, LANES), F32)
    for kk in range(TOP_K):
        rank = jnp.sum(jnp.where(lane_f == idxs[kk], before, 0.0), axis=-1, keepdims=True)
        sel = jnp.where(lane == kk, idxs[kk], sel)
        sel = jnp.where(lane == TOP_K + kk, rank, sel)
        gate = jnp.where(lane == kk, exps[kk] / denom, gate)
    sel_ref[...] = sel.astype(I32)
    gate_ref[...] = gate


def _merge_router(x, hg, o, z, wrg, wgdn, wout, gnw, fnw, rw, rb):
    t = x.shape[0]
    tm = min(TM_MERGE, t)
    idx = jnp.arange(tm)
    lstrict = (idx[:, None] > idx[None, :]).astype(BF16)
    full = lambda shape: pl.BlockSpec(shape, lambda i: (0,) * len(shape))
    row = lambda width, col=0: pl.BlockSpec((tm, width), lambda i: (i, col))
    return pl.pallas_call(
        _merge_kernel,
        grid=(t // tm,),
        in_specs=[
            row(D_MODEL), row(RG_WIDTH), row(VAL_DIM), row(VAL_DIM, Z_ZG), row(D_MODEL, Z_GA),
            row(D_MODEL, Z_GB),
            full((RG_WIDTH, D_MODEL)), full((VAL_DIM, D_MODEL)), full((D_MODEL, D_MODEL)),
            full((1, HEAD_DIM)), full((1, D_MODEL)), full((D_MODEL, LANES)), full((1, LANES)),
            full((tm, tm)),
        ],
        out_specs=[row(D_MODEL), row(D_MODEL), row(LANES), row(LANES), full((1, LANES))],
        out_shape=[
            jax.ShapeDtypeStruct((t, D_MODEL), F32),
            jax.ShapeDtypeStruct((t, D_MODEL), F32),
            jax.ShapeDtypeStruct((t, LANES), I32),
            jax.ShapeDtypeStruct((t, LANES), F32),
            jax.ShapeDtypeStruct((1, LANES), I32),
        ],
        scratch_shapes=[pltpu.VMEM((1, LANES), F32)],
        compiler_params=_cparams(("arbitrary",)),
        name="merge_router",
    )(x, hg, o, z, z, z, wrg, wgdn, wout, gnw, fnw, rw, rb, lstrict)


def _dest_kernel(sel_ref, pstart_ref, dest_ref):
    tm = sel_ref.shape[0]
    sel = sel_ref[...].astype(F32)
    lane = lax.broadcasted_iota(I32, (tm, LANES), 1)
    lane_f = lane.astype(F32)
    pstart = pstart_ref[...].astype(F32)
    dest = jnp.zeros((tm, LANES), F32)
    for kk in range(TOP_K):
        e = jnp.sum(jnp.where(lane == kk, sel, 0.0), axis=-1, keepdims=True)
        rank = jnp.sum(jnp.where(lane == TOP_K + kk, sel, 0.0), axis=-1, keepdims=True)
        base = jnp.sum(jnp.where(lane_f == e, pstart, 0.0), axis=-1, keepdims=True)
        dest = jnp.where(lane == kk, base + rank, dest)
    dest_ref[...] = dest.astype(I32)


def _dest_rows(sel, pstart):
    t = sel.shape[0]
    tm = min(1024, t)
    return pl.pallas_call(
        _dest_kernel,
        grid=(t // tm,),
        in_specs=[pl.BlockSpec((tm, LANES), lambda i: (i, 0)), pl.BlockSpec((1, LANES), lambda i: (0, 0))],
        out_specs=pl.BlockSpec((tm, LANES), lambda i: (i, 0)),
        out_shape=jax.ShapeDtypeStruct((t, LANES), I32),
        compiler_params=_cparams(("arbitrary",)),
        name="dest_rows",
    )(sel, pstart)


def _row_copy(src_ref, src_row, dst_ref, dst_row, sem):
    return pltpu.make_async_copy(src_ref.at[pl.ds(src_row, 1), :], dst_ref.at[pl.ds(dst_row, 1), :], sem)


def _dispatch_kernel(dest_ref, zflag_ref, hn_ref, xr_ref, zero_sc, sem, zsem):
    tm = hn_ref.shape[0]
    n_tiles = zflag_ref.shape[0]

    def zero_copy(tile):
        start = pl.multiple_of(tile * MOE_TILE, MOE_TILE)
        return pltpu.make_async_copy(zero_sc, xr_ref.at[pl.ds(start, MOE_TILE), :], zsem)

    @pl.when(pl.program_id(0) == 0)
    def _():
        zero_sc[...] = jnp.zeros_like(zero_sc)

        def start(tile, carry):
            @pl.when(zflag_ref[tile] > 0)
            def _():
                zero_copy(tile).start()
            return carry

        def wait(tile, carry):
            @pl.when(zflag_ref[tile] > 0)
            def _():
                zero_copy(tile).wait()
            return carry

        lax.fori_loop(0, n_tiles, start, 0)
        lax.fori_loop(0, n_tiles, wait, 0)

    def issue(tok, carry):
        for kk in range(TOP_K):
            _row_copy(hn_ref, tok, xr_ref, dest_ref[tok * TOP_K + kk], sem).start()
        return carry

    lax.fori_loop(0, tm, issue, 0)

    def drain(tok, carry):
        for kk in range(TOP_K):
            _row_copy(hn_ref, tok, xr_ref, dest_ref[tok * TOP_K + kk], sem).wait()
        return carry

    lax.fori_loop(0, tm, drain, 0)


def _dispatch(dest_flat, zflag, hn):
    t = hn.shape[0]
    tm = min(TM_ROWS, t)
    n_tiles = zflag.shape[0]
    return pl.pallas_call(
        _dispatch_kernel,
        grid=(t // tm,),
        in_specs=[
            pl.BlockSpec((tm * TOP_K,), lambda i: (i,), memory_space=pltpu.SMEM),
            pl.BlockSpec((n_tiles,), lambda i: (0,), memory_space=pltpu.SMEM),
            pl.BlockSpec((tm, D_MODEL), lambda i: (i, 0)),
        ],
        out_specs=pl.BlockSpec(memory_space=pl.ANY),
        out_shape=jax.ShapeDtypeStruct((n_tiles * MOE_TILE, D_MODEL), F32),
        scratch_shapes=[pltpu.VMEM((MOE_TILE, D_MODEL), F32), pltpu.SemaphoreType.DMA,
                        pltpu.SemaphoreType.DMA],
        compiler_params=_cparams(("arbitrary",)),
        name="dispatch",
    )(dest_flat, zflag, hn)


def _expert_kernel(te_ref, tv_ref, tf_ref, x_ref, wgu_ref, bgu_ref, wd_ref, bd_ref, y_ref,
                   wgu_sc, wd_sc):
    i = pl.program_id(0)

    @pl.when(tf_ref[i] > 0)
    def _():
        wgu_sc[...] = wgu_ref[0].astype(BF16)
        wd_sc[...] = wd_ref[0].astype(BF16)

    @pl.when(tv_ref[i] > 0)
    def _():
        gu = _dot(x_ref[...].astype(BF16), wgu_sc[...]) + bgu_ref[0]
        gate = jnp.minimum(gu[:, :D_EXPERT], SWIGLU_LIMIT)
        up = jnp.clip(gu[:, D_EXPERT:], -SWIGLU_LIMIT, SWIGLU_LIMIT)
        hid = (up + 1.0) * (gate * _sigmoid(SWIGLU_ALPHA * gate))
        y_ref[...] = _dot(hid.astype(BF16), wd_sc[...]) + bd_ref[0]

    @pl.when(tv_ref[i] == 0)
    def _():
        y_ref[...] = jnp.zeros_like(y_ref)


def _experts(tile_e, tile_valid, tile_first, x_rows, wgu, bgu, wd, bd):
    n_rows = x_rows.shape[0]
    n_tiles = n_rows // MOE_TILE
    grid_spec = pltpu.PrefetchScalarGridSpec(
        num_scalar_prefetch=3,
        grid=(n_tiles,),
        in_specs=[
            pl.BlockSpec((MOE_TILE, D_MODEL), lambda i, te, tv, tf: (i, 0)),
            pl.BlockSpec((1, D_MODEL, 2 * D_EXPERT), lambda i, te, tv, tf: (te[i], 0, 0)),
            pl.BlockSpec((1, 1, 2 * D_EXPERT), lambda i, te, tv, tf: (te[i], 0, 0)),
            pl.BlockSpec((1, D_EXPERT, D_MODEL), lambda i, te, tv, tf: (te[i], 0, 0)),
            pl.BlockSpec((1, 1, D_MODEL), lambda i, te, tv, tf: (te[i], 0, 0)),
        ],
        out_specs=pl.BlockSpec((MOE_TILE, D_MODEL), lambda i, te, tv, tf: (i, 0)),
        scratch_shapes=[pltpu.VMEM((D_MODEL, 2 * D_EXPERT), BF16), pltpu.VMEM((D_EXPERT, D_MODEL), BF16)],
    )
    return pl.pallas_call(
        _expert_kernel,
        grid_spec=grid_spec,
        out_shape=jax.ShapeDtypeStruct((n_rows, D_MODEL), F32),
        compiler_params=_cparams(("arbitrary",)),
        name="experts",
    )(tile_e, tile_valid, tile_first, x_rows, wgu, bgu, wd, bd)


def _combine_kernel(dest_ref, h_ref, gate_ref, nw_ref, yr_ref, out_ref, buf_sc, sem):
    tm = h_ref.shape[0]

    def row_copy(tok, kk):
        return _row_copy(yr_ref, dest_ref[tok * TOP_K + kk], buf_sc.at[kk], tok, sem)

    def issue(tok, carry):
        for kk in range(TOP_K):
            row_copy(tok, kk).start()
        return carry

    lax.fori_loop(0, tm, issue, 0)

    def drain(tok, carry):
        for kk in range(TOP_K):
            row_copy(tok, kk).wait()
        return carry

    lax.fori_loop(0, tm, drain, 0)

    gate = gate_ref[...]
    acc = h_ref[...]
    for kk in range(TOP_K):
        acc = acc + gate[:, kk:kk + 1] * buf_sc[kk]
    out_ref[...] = _rms(acc, nw_ref[...])


def _combine(dest_flat, h, gate, nw, y_rows):
    t = h.shape[0]
    tm = min(TM_ROWS, t)
    return pl.pallas_call(
        _combine_kernel,
        grid=(t // tm,),
        in_specs=[
            pl.BlockSpec((tm * TOP_K,), lambda i: (i,), memory_space=pltpu.SMEM),
            pl.BlockSpec((tm, D_MODEL), lambda i: (i, 0)),
            pl.BlockSpec((tm, LANES), lambda i: (i, 0)),
            pl.BlockSpec((1, D_MODEL), lambda i: (0, 0)),
            pl.BlockSpec(memory_space=pl.ANY),
        ],
        out_specs=pl.BlockSpec((tm, D_MODEL), lambda i: (i, 0)),
        out_shape=jax.ShapeDtypeStruct((t, D_MODEL), F32),
        scratch_shapes=[pltpu.VMEM((TOP_K, tm, D_MODEL), F32), pltpu.SemaphoreType.DMA],
        compiler_params=_cparams(("arbitrary",)),
        name="combine",
    )(dest_flat, h, gate, nw, y_rows)


def _pad_lanes(a, offset, axis):
    n = a.shape[axis]
    pads = [(0, 0)] * a.ndim
    pads[axis] = (offset, LANES - offset - n)
    return jnp.pad(a, pads)


def _layer(x, norm_mix_w, w_in, rg_conv_w, rg_conv_b, rg_gate_a_w, rg_gate_a_b, rg_gate_x_w,
           rg_gate_x_b, rg_lambda, gdn_conv_w, gdn_a_log, gdn_dt_bias, gdn_norm_w, w_branch_rg,
           w_branch_gdn, w_out, norm_ffn_w, router_w, router_b, moe_w_gate_up, moe_b_gate_up,
           moe_w_down, moe_b_down, out_norm_w):
    t = x.shape[0]
    row = lambda a: a.reshape(1, -1).astype(F32)

    small_lo = 2 * RG_WIDTH + 2 * KEY_DIM + 2 * VAL_DIM
    small_hi = small_lo + 2 * V_HEADS
    w_z = jnp.concatenate([w_in[:, :small_lo], w_in[:, small_hi:]], axis=1).astype(BF16)
    w_ba = _pad_lanes(w_in[:, small_lo:small_hi], 0, 1)
    z, ba, bat = _inproj(x, row(norm_mix_w), w_z, w_ba, w_ba.T)

    hg = _rg_branch(z, rg_conv_w, row(rg_conv_b), rg_gate_a_w.astype(BF16), row(rg_gate_a_b),
                    rg_gate_x_w.astype(BF16), row(rg_gate_x_b), row(rg_lambda))

    q, k, v = _gdn_prep(z, gdn_conv_w)
    arow = jnp.stack([_pad_lanes(gdn_a_log, V_HEADS, 0), _pad_lanes(gdn_dt_bias, V_HEADS, 0)])
    o = _gdn_core(q, k, v, ba, bat, arow, arow.T)

    rw = _pad_lanes(router_w, 0, 1)
    rb = _pad_lanes(router_b, 0, 0).reshape(1, LANES)
    h, hn, sel, gate, counts = _merge_router(
        x, hg, o, z, w_branch_rg.astype(BF16), w_branch_gdn.astype(BF16), w_out.astype(BF16),
        row(gdn_norm_w), row(norm_ffn_w), rw, rb)

    counts = counts[0, :N_EXPERTS]
    padded = ((counts + MOE_TILE - 1) // MOE_TILE) * MOE_TILE
    pends = jnp.cumsum(padded)
    pstarts = pends - padded
    n_tiles = (t * TOP_K) // MOE_TILE + N_EXPERTS
    tile_start = jnp.arange(n_tiles, dtype=I32) * MOE_TILE
    tile_valid = (tile_start < pends[-1]).astype(I32)
    tile_e = jnp.minimum(jnp.searchsorted(pends, tile_start, side="right"), N_EXPERTS - 1).astype(I32)
    prev_e = jnp.concatenate([jnp.full((1,), -1, I32), tile_e[:-1]])
    tile_first = jnp.logical_and(tile_e != prev_e, tile_valid > 0).astype(I32)
    group_end = jnp.any(jnp.logical_and((tile_start + MOE_TILE)[:, None] == pends[None, :],
                                        padded[None, :] > 0), axis=1)
    zflag = jnp.logical_or(group_end, tile_valid == 0).astype(I32)

    dest = _dest_rows(sel, _pad_lanes(pstarts.astype(I32), 0, 0).reshape(1, LANES))
    dest_flat = dest[:, :TOP_K].reshape(t * TOP_K)

    x_rows = _dispatch(dest_flat, zflag, hn)
    y_rows = _experts(tile_e, tile_valid, tile_first, x_rows, moe_w_gate_up,
                      moe_b_gate_up.reshape(N_EXPERTS, 1, 2 * D_EXPERT), moe_w_down,
                      moe_b_down.reshape(N_EXPERTS, 1, D_MODEL))

    return _combine(dest_flat, h, gate, row(out_norm_w), y_rows)


def kernel(x, norm_mix_w, w_in, rg_conv_w, rg_conv_b, rg_gate_a_w, rg_gate_a_b, rg_gate_x_w,
           rg_gate_x_b, rg_lambda, gdn_conv_w, gdn_A_log, gdn_dt_bias, gdn_norm_w, w_branch_rg,
           w_branch_gdn, w_out, norm_ffn_w, router_w, router_b, moe_w_gate_up, moe_b_gate_up,
           moe_w_down, moe_b_down, norm_final_w):
    assert x.shape[0] == 1 and norm_mix_w.shape[0] == 1
    out = _layer(x[0], norm_mix_w[0], w_in[0], rg_conv_w[0], rg_conv_b[0], rg_gate_a_w[0],
                 rg_gate_a_b[0], rg_gate_x_w[0], rg_gate_x_b[0], rg_lambda[0], gdn_conv_w[0],
                 gdn_A_log[0], gdn_dt_bias[0], gdn_norm_w[0], w_branch_rg[0], w_branch_gdn[0],
                 w_out[0], norm_ffn_w[0], router_w[0], router_b[0], moe_w_gate_up[0],
                 moe_b_gate_up[0], moe_w_down[0], moe_b_down[0], norm_final_w)
    return out[None]
```

```python
import functools

import jax
import jax.numpy as jnp
from jax import lax
from jax.experimental import pallas as pl
from jax.experimental.pallas import tpu as pltpu

F32 = jnp.float32
BF16 = jnp.bfloat16
I32 = jnp.int32
HI = lax.Precision.HIGHEST

D_MODEL = 1024
RG_WIDTH = 1024
RG_BLOCKS = 4
RG_BLOCK_DIM = RG_WIDTH // RG_BLOCKS
RG_C = 8.0
CONV_K = 4
QK_HEADS = 8
V_HEADS = 16
HEAD_DIM = 128
KEY_DIM = QK_HEADS * HEAD_DIM
VAL_DIM = V_HEADS * HEAD_DIM
CHUNK = 64
N_EXPERTS = 32
TOP_K = 4
D_EXPERT = 1024
SWIGLU_ALPHA = 1.702
SWIGLU_LIMIT = 7.0
EPS = 1e-6

LANES = 128
SUBLANES = 8
VMEM_LIMIT = 56 * 1024 * 1024

Z_RGX, Z_RGY, Z_Q, Z_K = 0, 1, 2, 3
Z_V, Z_ZG = 2, 3
Z_GA, Z_GB = 8, 9
Z_WIDTH = 10 * 1024

TM_INPROJ = 1024
TN_INPROJ = 1024
TM_RG = 256
TM_PREP = 256
GDN_HG = 16
GDN_NCH = 2
TM_MERGE = 512
TM_ROWS = 256
MOE_TILE = 512


def _cparams(sem):
    return pltpu.CompilerParams(dimension_semantics=sem, vmem_limit_bytes=VMEM_LIMIT)


def _softplus(x):
    return jnp.maximum(x, 0.0) + jnp.log1p(jnp.exp(-jnp.abs(x)))


def _sigmoid(x):
    return 1.0 / (1.0 + jnp.exp(-x))


def _silu(x):
    return x * _sigmoid(x)


def _gelu_tanh(x):
    c = 0.7978845608028654
    return 0.5 * x * (1.0 + jnp.tanh(c * (x + 0.044715 * (x * x * x))))


def _rms(x, w):
    return x * lax.rsqrt(jnp.mean(x * x, axis=-1, keepdims=True) + EPS) * w


def _dot(a, b):
    return jnp.dot(a, b, preferred_element_type=F32)


def _dot_hi(a, b):
    return jnp.dot(a, b, precision=HI, preferred_element_type=F32)


def _bf16_terms(x):
    t0 = x.astype(BF16)
    r = x - t0.astype(F32)
    t1 = r.astype(BF16)
    t2 = (r - t1.astype(F32)).astype(BF16)
    return t0, t1, t2


def _dot_nt(a, b, precision=None):
    return lax.dot_general(a, b, (((1,), (1,)), ((), ())), precision=precision,
                           preferred_element_type=F32)


def _dot_tn(a, b):
    return lax.dot_general(a, b, (((0,), (0,)), ((), ())), preferred_element_type=F32)


def _inproj_kernel(x_ref, nw_ref, w_ref, wba_ref, z_ref, ba_ref, bat_ref, xn_sc):
    @pl.when(pl.program_id(1) == 0)
    def _():
        xn = _rms(x_ref[...], nw_ref[...])
        x_hi = xn.astype(BF16)
        xn_sc[...] = x_hi
        x_lo = (xn - x_hi.astype(F32)).astype(BF16)
        wba = wba_ref[...]
        hi = _dot(x_hi, wba)
        ba = hi[:, :LANES] + hi[:, LANES:] + _dot(x_lo, wba[:, :LANES])
        ba_ref[...] = ba
        bat_ref[...] = ba.T

    z_ref[...] = _dot(xn_sc[...], w_ref[...]).astype(BF16)


def _inproj(x, nw, w_z, w_ba):
    t = x.shape[0]
    tm, tn = min(TM_INPROJ, t), TN_INPROJ
    return pl.pallas_call(
        _inproj_kernel,
        grid=(t // tm, Z_WIDTH // tn),
        in_specs=[
            pl.BlockSpec((tm, D_MODEL), lambda i, j: (i, 0)),
            pl.BlockSpec((1, D_MODEL), lambda i, j: (0, 0)),
            pl.BlockSpec((D_MODEL, tn), lambda i, j: (0, j)),
            pl.BlockSpec((D_MODEL, 2 * LANES), lambda i, j: (0, 0)),
        ],
        out_specs=[
            pl.BlockSpec((tm, tn), lambda i, j: (i, j)),
            pl.BlockSpec((tm, LANES), lambda i, j: (i, 0)),
            pl.BlockSpec((LANES, tm), lambda i, j: (0, i)),
        ],
        out_shape=[
            jax.ShapeDtypeStruct((t, Z_WIDTH), BF16),
            jax.ShapeDtypeStruct((t, LANES), F32),
            jax.ShapeDtypeStruct((LANES, t), F32),
        ],
        scratch_shapes=[pltpu.VMEM((tm, D_MODEL), BF16)],
        compiler_params=_cparams(("arbitrary", "arbitrary")),
        name="inproj",
    )(x, nw, w_z, w_ba)


def _causal_conv(x, prev, cw):
    tm = x.shape[0]
    xe = jnp.concatenate([prev, x], axis=0)
    off = SUBLANES - (CONV_K - 1)
    acc = cw[0:1, :] * xe[off:off + tm, :]
    for k in range(1, CONV_K):
        acc = acc + cw[k:k + 1, :] * xe[off + k:off + k + tm, :]
    return acc


def _rg_kernel(x_ref, y_ref, cw_ref, cb_ref, wa_ref, ba_ref, wx_ref, bx_ref, lam_ref, out_ref,
               prev_sc, h_sc):
    i = pl.program_id(0)
    tm = x_ref.shape[0]

    @pl.when(i == 0)
    def _():
        prev_sc[...] = jnp.zeros_like(prev_sc)
        h_sc[...] = jnp.zeros_like(h_sc)

    x = x_ref[...].astype(F32)
    xa = _causal_conv(x, prev_sc[...], cw_ref[...]) + cb_ref[...]
    prev_sc[...] = x[tm - SUBLANES:, :]

    xab = xa.astype(BF16)
    r_parts, i_parts = [], []
    for blk in range(RG_BLOCKS):
        xs = xab[:, blk * RG_BLOCK_DIM:(blk + 1) * RG_BLOCK_DIM]
        r_parts.append(_dot(xs, wa_ref[blk]))
        i_parts.append(_dot(xs, wx_ref[blk]))
    r = _sigmoid(jnp.concatenate(r_parts, axis=1) + ba_ref[...])
    ig = _sigmoid(jnp.concatenate(i_parts, axis=1) + bx_ref[...])

    log_a = (-RG_C) * r * _softplus(-lam_ref[...])
    a = jnp.exp(log_a)
    mult = jnp.sqrt(jnp.maximum(1.0 - a * a, 0.0))
    rows = lax.broadcasted_iota(I32, (tm, 1), 0)
    mult = jnp.where(jnp.logical_and(rows == 0, i == 0), 1.0, mult)
    b = mult * (ig * xa)

    s = 1
    while s < tm:
        keep = rows >= s
        a_sh = pltpu.roll(a, s, axis=0)
        b_sh = pltpu.roll(b, s, axis=0)
        b = b + jnp.where(keep, a * b_sh, 0.0)
        a = jnp.where(keep, a * a_sh, a)
        s *= 2
    h = b + a * h_sc[...]
    h_sc[...] = h[tm - 1:tm, :]
    out_ref[...] = (h * _gelu_tanh(y_ref[...].astype(F32))).astype(BF16)


def _rg_branch(z, cw, cb, wa, ba, wx, bx, lam):
    t = z.shape[0]
    tm = min(TM_RG, t)
    full = lambda shape: pl.BlockSpec(shape, lambda i: (0,) * len(shape))
    return pl.pallas_call(
        _rg_kernel,
        grid=(t // tm,),
        in_specs=[
            pl.BlockSpec((tm, RG_WIDTH), lambda i: (i, Z_RGX)),
            pl.BlockSpec((tm, RG_WIDTH), lambda i: (i, Z_RGY)),
            full((CONV_K, RG_WIDTH)),
            full((1, RG_WIDTH)),
            full((RG_BLOCKS, RG_BLOCK_DIM, RG_BLOCK_DIM)),
            full((1, RG_WIDTH)),
            full((RG_BLOCKS, RG_BLOCK_DIM, RG_BLOCK_DIM)),
            full((1, RG_WIDTH)),
            full((1, RG_WIDTH)),
        ],
        out_specs=pl.BlockSpec((tm, RG_WIDTH), lambda i: (i, 0)),
        out_shape=jax.ShapeDtypeStruct((t, RG_WIDTH), BF16),
        scratch_shapes=[pltpu.VMEM((SUBLANES, RG_WIDTH), F32), pltpu.VMEM((1, RG_WIDTH), F32)],
        compiler_params=_cparams(("arbitrary",)),
        name="rg_branch",
    )(z, z, cw, cb, wa, ba, wx, bx, lam)


def _l2norm_heads(x, n_heads):
    parts = []
    for h in range(n_heads):
        xh = x[:, h * HEAD_DIM:(h + 1) * HEAD_DIM]
        parts.append(xh * lax.rsqrt(jnp.sum(xh * xh, axis=-1, keepdims=True) + EPS))
    return jnp.concatenate(parts, axis=1)


def _gdnprep_kernel(q_ref, k_ref, v_ref, cw_ref, qo_ref, ko_ref, vo_ref, prev_sc):
    tm = q_ref.shape[0]

    @pl.when(pl.program_id(0) == 0)
    def _():
        prev_sc[...] = jnp.zeros_like(prev_sc)

    def conv_silu(ref, lo, width):
        x = ref[...].astype(F32)
        y = _causal_conv(x, prev_sc[:, lo:lo + width], cw_ref[:, lo:lo + width])
        prev_sc[:, lo:lo + width] = x[tm - SUBLANES:, :]
        return _silu(y)

    q = conv_silu(q_ref, 0, KEY_DIM)
    qo_ref[...] = _l2norm_heads(q, QK_HEADS).astype(BF16)
    k = conv_silu(k_ref, KEY_DIM, KEY_DIM)
    ko_ref[...] = _l2norm_heads(k, QK_HEADS).astype(BF16)
    v = conv_silu(v_ref, 2 * KEY_DIM, VAL_DIM)
    vo_ref[...] = v.astype(BF16)


def _gdn_prep(z, cw):
    t = z.shape[0]
    tm = min(TM_PREP, t)
    return pl.pallas_call(
        _gdnprep_kernel,
        grid=(t // tm,),
        in_specs=[
            pl.BlockSpec((tm, KEY_DIM), lambda i: (i, Z_Q)),
            pl.BlockSpec((tm, KEY_DIM), lambda i: (i, Z_K)),
            pl.BlockSpec((tm, VAL_DIM), lambda i: (i, Z_V)),
            pl.BlockSpec((CONV_K, 2 * KEY_DIM + VAL_DIM), lambda i: (0, 0)),
        ],
        out_specs=[
            pl.BlockSpec((tm, KEY_DIM), lambda i: (i, 0)),
            pl.BlockSpec((tm, KEY_DIM), lambda i: (i, 0)),
            pl.BlockSpec((tm, VAL_DIM), lambda i: (i, 0)),
        ],
        out_shape=[
            jax.ShapeDtypeStruct((t, KEY_DIM), BF16),
            jax.ShapeDtypeStruct((t, KEY_DIM), BF16),
            jax.ShapeDtypeStruct((t, VAL_DIM), BF16),
        ],
        scratch_shapes=[pltpu.VMEM((SUBLANES, 2 * KEY_DIM + VAL_DIM), F32)],
        compiler_params=_cparams(("arbitrary",)),
        name="gdn_prep",
    )(z, z, z, cw)


def _unit_lower_inverses(mats):
    c = mats[0].shape[0]
    ri = lax.broadcasted_iota(I32, (c, c), 0)
    ci = lax.broadcasted_iota(I32, (c, c), 1)
    eye = jnp.where(ri == ci, 1.0, 0.0)
    ps = [eye - a for a in mats]
    xbs = [a.astype(BF16) for a in mats]
    xbs = [_dot(xb, xb).astype(BF16) for xb in xbs]
    power = 2
    while power < c:
        ps = [p + _dot(p.astype(BF16), xb) for p, xb in zip(ps, xbs)]
        power *= 2
        if power < c:
            xbs = [_dot(xb, xb).astype(BF16) for xb in xbs]
    return ps


def _gdn_kernel(q_ref, k_ref, v_ref, ba_ref, bat_ref, arow_ref, acol_ref, msum_ref,
                o_ref, s_sc, gt_sc, *, hg_n, nch):
    grp = pl.program_id(0)
    c = CHUNK
    tile = nch * c
    scale = HEAD_DIM ** -0.5

    @pl.when(pl.program_id(1) == 0)
    def _():
        s_sc[...] = jnp.zeros_like(s_sc)

    ba = ba_ref[...]
    arow = arow_ref[...]
    g_all = -jnp.exp(arow[0:1, :]) * _softplus(ba + arow[1:2, :])
    beta_all = _sigmoid(ba)
    msum = msum_ref[...]
    sums = sum(_dot(msum, term) for term in _bf16_terms(g_all))
    gcs = sums[:tile, :]
    gls = sums[tile:, :]
    egc = jnp.exp(gcs)
    ekd = jnp.exp(gls - gcs)
    acol = acol_ref[...]
    g_t = -jnp.exp(acol[:, 0:1]) * _softplus(bat_ref[...] + acol[:, 1:2])
    gt_sc[...] = sum(_dot_nt(term, msum[:tile, :]) for term in _bf16_terms(g_t))

    lane = lax.broadcasted_iota(I32, (tile, LANES), 1)
    ri = lax.broadcasted_iota(I32, (c, c), 0)
    ci = lax.broadcasted_iota(I32, (c, c), 1)
    causal = ri >= ci
    strict = ri > ci

    def column(arr, lane_idx):
        return jnp.sum(jnp.where(lane == lane_idx, arr, 0.0), axis=1, keepdims=True)

    cols = []
    for j in range(hg_n):
        head = grp * hg_n + j
        b_col = column(beta_all, head)
        egc_col = column(egc, V_HEADS + head)
        cols.append(dict(
            b=b_col, gcs=column(gcs, V_HEADS + head), egc=egc_col,
            ekd=column(ekd, V_HEADS + head), be=b_col * egc_col,
            g_row=gt_sc[pl.ds(V_HEADS + head, 1), :]))

    pairs = [(n, j) for n in range(nch) for j in range(hg_n)]
    rows_of = lambda n: slice(n * c, (n + 1) * c)
    head_cols = lambda j: slice(j * HEAD_DIM, (j + 1) * HEAD_DIM)
    kq = {}
    for n in range(nch):
        for qh in range(hg_n // 2):
            k = k_ref[rows_of(n), head_cols(qh)]
            q = q_ref[rows_of(n), head_cols(qh)]
            kq[(n, qh)] = dict(kk=_dot_nt(k, k), qk=_dot_nt(q, k), kf=k.astype(F32), qf=q.astype(F32))
    decays, a_mats = {}, []
    for n, j in pairs:
        col = cols[j]
        diff = col["gcs"][rows_of(n), :] - col["g_row"][:, rows_of(n)]
        decay = jnp.where(causal, jnp.exp(jnp.where(causal, diff, 0.0)), 0.0)
        decays[(n, j)] = decay
        a_mats.append(jnp.where(strict, col["b"][rows_of(n), :] * kq[(n, j // 2)]["kk"] * decay, 0.0))
    t_mats = _unit_lower_inverses(a_mats)
    uws, attns, q_decs, k_decs = {}, {}, {}, {}
    for (n, j), t_mat in zip(pairs, t_mats):
        col, kqn = cols[j], kq[(n, j // 2)]
        v = v_ref[rows_of(n), head_cols(j)].astype(F32)
        rhs = jnp.concatenate([(v * col["b"][rows_of(n), :]).astype(BF16),
                               (kqn["kf"] * col["be"][rows_of(n), :]).astype(BF16)], axis=1)
        uws[(n, j)] = _dot(t_mat.astype(BF16), rhs)
        attns[(n, j)] = (kqn["qk"] * decays[(n, j)]).astype(BF16)
        q_decs[(n, j)] = (kqn["qf"] * col["egc"][rows_of(n), :]).astype(BF16)
        k_decs[(n, j)] = (kqn["kf"] * col["ekd"][rows_of(n), :]).astype(BF16)

    states = [s_sc[j] for j in range(hg_n)]
    for n in range(nch):
        heads = range(hg_n)
        ws_qs = [_dot(jnp.concatenate([uws[(n, j)][:, HEAD_DIM:].astype(BF16), q_decs[(n, j)]], axis=0),
                      states[j].astype(BF16)) for j in heads]
        v_news = [(uws[(n, j)][:, :HEAD_DIM] - ws_qs[j][:c, :]).astype(BF16) for j in heads]
        outs = [(ws_qs[j][c:, :] + _dot(attns[(n, j)], v_news[j])) * scale for j in heads]
        for j in heads:
            o_ref[rows_of(n), head_cols(j)] = outs[j].astype(BF16)
        egls = [jnp.exp(cols[j]["g_row"][:, (n + 1) * c - 1:(n + 1) * c]) for j in heads]
        states = [states[j] * egls[j] + _dot_tn(k_decs[(n, j)], v_news[j]) for j in heads]
    for j in range(hg_n):
        s_sc[j] = states[j]


def _gdn_core(q, k, v, ba, bat, arow, acol):
    t = q.shape[0]
    hg_n, nch = GDN_HG, GDN_NCH
    tile = nch * CHUNK
    idx = jnp.arange(tile)
    same = (idx[:, None] // CHUNK) == (idx[None, :] // CHUNK)
    mtri = jnp.logical_and(same, idx[:, None] >= idx[None, :])
    msum = jnp.concatenate([mtri, same], axis=0).astype(BF16)
    qw = (hg_n // 2) * HEAD_DIM
    vw = hg_n * HEAD_DIM
    return pl.pallas_call(
        functools.partial(_gdn_kernel, hg_n=hg_n, nch=nch),
        grid=(V_HEADS // hg_n, t // tile),
        in_specs=[
            pl.BlockSpec((tile, qw), lambda g, n: (n, g)),
            pl.BlockSpec((tile, qw), lambda g, n: (n, g)),
            pl.BlockSpec((tile, vw), lambda g, n: (n, g)),
            pl.BlockSpec((tile, LANES), lambda g, n: (n, 0)),
            pl.BlockSpec((LANES, tile), lambda g, n: (0, n)),
            pl.BlockSpec((2, LANES), lambda g, n: (0, 0)),
            pl.BlockSpec((LANES, 2), lambda g, n: (0, 0)),
            pl.BlockSpec((2 * tile, tile), lambda g, n: (0, 0)),
        ],
        out_specs=pl.BlockSpec((tile, vw), lambda g, n: (n, g)),
        out_shape=jax.ShapeDtypeStruct((t, VAL_DIM), BF16),
        scratch_shapes=[pltpu.VMEM((hg_n, HEAD_DIM, HEAD_DIM), F32), pltpu.VMEM((LANES, tile), F32)],
        compiler_params=_cparams(("arbitrary", "arbitrary")),
        name="gdn_core",
    )(q, k, v, ba, bat, arow, acol, msum)


def _merge_kernel(x_ref, hg_ref, o_ref, zg_ref, ga_ref, gb_ref, wrg_ref, wgdn_ref, wout_ref,
                  gnw_ref, fnw_ref, rw_ref, rb_ref, lstrict_ref,
                  h_ref, hn_ref, sel_ref, gate_ref, cnt_ref, carry_sc):
    tm = x_ref.shape[0]

    @pl.when(pl.program_id(0) == 0)
    def _():
        carry_sc[...] = jnp.zeros_like(carry_sc)

    ya = _dot(hg_ref[...], wrg_ref[...])
    o = o_ref[...].astype(F32)
    zg = zg_ref[...].astype(F32)
    gnw = gnw_ref[...]
    parts = []
    for h in range(V_HEADS):
        sl = slice(h * HEAD_DIM, (h + 1) * HEAD_DIM)
        parts.append((_rms(o[:, sl], gnw) * _silu(zg[:, sl])).astype(BF16))
    yb = _dot(jnp.concatenate(parts, axis=1), wgdn_ref[...])
    mix = _sigmoid(ga_ref[...].astype(F32)) * ya + _sigmoid(gb_ref[...].astype(F32)) * yb
    h = x_ref[...] + _dot(mix.astype(BF16), wout_ref[...])
    h_ref[...] = h

    hn = _rms(h, fnw_ref[...])
    hn_ref[...] = hn
    lane = lax.broadcasted_iota(I32, (tm, LANES), 1)
    lane_f = lane.astype(F32)
    neg_inf = float("-inf")
    logits = jnp.where(lane < N_EXPERTS, _dot_hi(hn, rw_ref[...]) + rb_ref[...], neg_inf)
    vals, idxs = [], []
    multi_hot = jnp.zeros((tm, LANES), F32)
    for _ in range(TOP_K):
        m = jnp.max(logits, axis=-1, keepdims=True)
        first = jnp.min(jnp.where(logits == m, lane_f, float(LANES)), axis=-1, keepdims=True)
        hit = lane_f == first
        multi_hot = multi_hot + jnp.where(hit, 1.0, 0.0)
        logits = jnp.where(hit, neg_inf, logits)
        vals.append(m)
        idxs.append(first)
    exps = [jnp.exp(v - vals[0]) for v in vals]
    denom = exps[0] + exps[1] + exps[2] + exps[3]

    before = _dot(lstrict_ref[...], multi_hot.astype(BF16)) + carry_sc[...]
    carry_sc[...] = carry_sc[...] + jnp.sum(multi_hot, axis=0, keepdims=True)
    cnt_ref[...] = carry_sc[...].astype(I32)

    sel = jnp.zeros((tm, LANES), F32)
    gate = jnp.zeros((tm, LANES), F32)
    for kk in range(TOP_K):
        rank = jnp.sum(jnp.where(lane_f == idxs[kk], before, 0.0), axis=-1, keepdims=True)
        sel = jnp.where(lane == kk, idxs[kk], sel)
        sel = jnp.where(lane == TOP_K + kk, rank, sel)
        gate = jnp.where(lane == kk, exps[kk] / denom, gate)
    sel_ref[...] = sel.astype(I32)
    gate_ref[...] = gate


def _merge_router(x, hg, o, z, wrg, wgdn, wout, gnw, fnw, rw, rb):
    t = x.shape[0]
    tm = min(TM_MERGE, t)
    idx = jnp.arange(tm)
    lstrict = (idx[:, None] > idx[None, :]).astype(BF16)
    full = lambda shape: pl.BlockSpec(shape, lambda i: (0,) * len(shape))
    row = lambda width, col=0: pl.BlockSpec((tm, width), lambda i: (i, col))
    return pl.pallas_call(
        _merge_kernel,
        grid=(t // tm,),
        in_specs=[
            row(D_MODEL), row(RG_WIDTH), row(VAL_DIM), row(VAL_DIM, Z_ZG), row(D_MODEL, Z_GA),
            row(D_MODEL, Z_GB),
            full((RG_WIDTH, D_MODEL)), full((VAL_DIM, D_MODEL)), full((D_MODEL, D_MODEL)),
            full((1, HEAD_DIM)), full((1, D_MODEL)), full((D_MODEL, LANES)), full((1, LANES)),
            full((tm, tm)),
        ],
        out_specs=[row(D_MODEL), row(D_MODEL), row(LANES), row(LANES), full((1, LANES))],
        out_shape=[
            jax.ShapeDtypeStruct((t, D_MODEL), F32),
            jax.ShapeDtypeStruct((t, D_MODEL), F32),
            jax.ShapeDtypeStruct((t, LANES), I32),
            jax.ShapeDtypeStruct((t, LANES), F32),
            jax.ShapeDtypeStruct((1, LANES), I32),
        ],
        scratch_shapes=[pltpu.VMEM((1, LANES), F32)],
        compiler_params=_cparams(("arbitrary",)),
        name="merge_router",
    )(x, hg, o, z, z, z, wrg, wgdn, wout, gnw, fnw, rw, rb, lstrict)


def _dest_kernel(sel_ref, pstart_ref, dest_ref):
    tm = sel_ref.shape[0]
    sel = sel_ref[...].astype(F32)
    lane = lax.broadcasted_iota(I32, (tm, LANES), 1)
    lane_f = lane.astype(F32)
    pstart = pstart_ref[...].astype(F32)
    dest = jnp.zeros((tm, LANES), F32)
    for kk in range(TOP_K):
        e = jnp.sum(jnp.where(lane == kk, sel, 0.0), axis=-1, keepdims=True)
        rank = jnp.sum(jnp.where(lane == TOP_K + kk, sel, 0.0), axis=-1, keepdims=True)
        base = jnp.sum(jnp.where(lane_f == e, pstart, 0.0), axis=-1, keepdims=True)
        dest = jnp.where(lane == kk, base + rank, dest)
    dest_ref[...] = dest.astype(I32)


def _dest_rows(sel, pstart):
    t = sel.shape[0]
    tm = min(1024, t)
    return pl.pallas_call(
        _dest_kernel,
        grid=(t // tm,),
        in_specs=[pl.BlockSpec((tm, LANES), lambda i: (i, 0)), pl.BlockSpec((1, LANES), lambda i: (0, 0))],
        out_specs=pl.BlockSpec((tm, LANES), lambda i: (i, 0)),
        out_shape=jax.ShapeDtypeStruct((t, LANES), I32),
        compiler_params=_cparams(("arbitrary",)),
        name="dest_rows",
    )(sel, pstart)


def _row_copy(src_ref, src_row, dst_ref, dst_row, sem):
    return pltpu.make_async_copy(src_ref.at[pl.ds(src_row, 1), :], dst_ref.at[pl.ds(dst_row, 1), :], sem)


def _dispatch_kernel(dest_ref, zflag_ref, hn_ref, xr_ref, zero_sc, sem, zsem):
    tm = hn_ref.shape[0]
    n_tiles = zflag_ref.shape[0]

    def zero_copy(tile):
        start = pl.multiple_of(tile * MOE_TILE, MOE_TILE)
        return pltpu.make_async_copy(zero_sc, xr_ref.at[pl.ds(start, MOE_TILE), :], zsem)

    @pl.when(pl.program_id(0) == 0)
    def _():
        zero_sc[...] = jnp.zeros_like(zero_sc)

        def start(tile, carry):
            @pl.when(zflag_ref[tile] > 0)
            def _():
                zero_copy(tile).start()
            return carry

        def wait(tile, carry):
            @pl.when(zflag_ref[tile] > 0)
            def _():
                zero_copy(tile).wait()
            return carry

        lax.fori_loop(0, n_tiles, start, 0)
        lax.fori_loop(0, n_tiles, wait, 0)

    def issue(tok, carry):
        for kk in range(TOP_K):
            _row_copy(hn_ref, tok, xr_ref, dest_ref[tok * TOP_K + kk], sem).start()
        return carry

    lax.fori_loop(0, tm, issue, 0)

    def drain(tok, carry):
        for kk in range(TOP_K):
            _row_copy(hn_ref, tok, xr_ref, dest_ref[tok * TOP_K + kk], sem).wait()
        return carry

    lax.fori_loop(0, tm, drain, 0)


def _dispatch(dest_flat, zflag, hn):
    t = hn.shape[0]
    tm = min(TM_ROWS, t)
    n_tiles = zflag.shape[0]
    return pl.pallas_call(
        _dispatch_kernel,
        grid=(t // tm,),
        in_specs=[
            pl.BlockSpec((tm * TOP_K,), lambda i: (i,), memory_space=pltpu.SMEM),
            pl.BlockSpec((n_tiles,), lambda i: (0,), memory_space=pltpu.SMEM),
            pl.BlockSpec((tm, D_MODEL), lambda i: (i, 0)),
        ],
        out_specs=pl.BlockSpec(memory_space=pl.ANY),
        out_shape=jax.ShapeDtypeStruct((n_tiles * MOE_TILE, D_MODEL), F32),
        scratch_shapes=[pltpu.VMEM((MOE_TILE, D_MODEL), F32), pltpu.SemaphoreType.DMA,
                        pltpu.SemaphoreType.DMA],
        compiler_params=_cparams(("arbitrary",)),
        name="dispatch",
    )(dest_flat, zflag, hn)


def _expert_kernel(te_ref, tv_ref, tf_ref, x_ref, wgu_ref, bgu_ref, wd_ref, bd_ref, y_ref,
                   wgu_sc, wd_sc):
    i = pl.program_id(0)

    @pl.when(tf_ref[i] > 0)
    def _():
        wgu_sc[...] = wgu_ref[0].astype(BF16)
        wd_sc[...] = wd_ref[0].astype(BF16)

    @pl.when(tv_ref[i] > 0)
    def _():
        gu = _dot(x_ref[...].astype(BF16), wgu_sc[...]) + bgu_ref[0]
        gate = jnp.minimum(gu[:, :D_EXPERT], SWIGLU_LIMIT)
        up = jnp.clip(gu[:, D_EXPERT:], -SWIGLU_LIMIT, SWIGLU_LIMIT)
        hid = (up + 1.0) * (gate * _sigmoid(SWIGLU_ALPHA * gate))
        y_ref[...] = _dot(hid.astype(BF16), wd_sc[...]) + bd_ref[0]

    @pl.when(tv_ref[i] == 0)
    def _():
        y_ref[...] = jnp.zeros_like(y_ref)


def _experts(tile_e, tile_valid, tile_first, x_rows, wgu, bgu, wd, bd):
    n_rows = x_rows.shape[0]
    n_tiles = n_rows // MOE_TILE
    grid_spec = pltpu.PrefetchScalarGridSpec(
        num_scalar_prefetch=3,
        grid=(n_tiles,),
        in_specs=[
            pl.BlockSpec((MOE_TILE, D_MODEL), lambda i, te, tv, tf: (i, 0)),
            pl.BlockSpec((1, D_MODEL, 2 * D_EXPERT), lambda i, te, tv, tf: (te[i], 0, 0)),
            pl.BlockSpec((1, 1, 2 * D_EXPERT), lambda i, te, tv, tf: (te[i], 0, 0)),
            pl.BlockSpec((1, D_EXPERT, D_MODEL), lambda i, te, tv, tf: (te[i], 0, 0)),
            pl.BlockSpec((1, 1, D_MODEL), lambda i, te, tv, tf: (te[i], 0, 0)),
        ],
        out_specs=pl.BlockSpec((MOE_TILE, D_MODEL), lambda i, te, tv, tf: (i, 0)),
        scratch_shapes=[pltpu.VMEM((D_MODEL, 2 * D_EXPERT), BF16), pltpu.VMEM((D_EXPERT, D_MODEL), BF16)],
    )
    return pl.pallas_call(
        _expert_kernel,
        grid_spec=grid_spec,
        out_shape=jax.ShapeDtypeStruct((n_rows, D_MODEL), F32),
        compiler_params=_cparams(("arbitrary",)),
        name="experts",
    )(tile_e, tile_valid, tile_first, x_rows, wgu, bgu, wd, bd)


def _combine_kernel(dest_ref, h_ref, gate_ref, nw_ref, yr_ref, out_ref, buf_sc, sem):
    tm = h_ref.shape[0]

    def row_copy(tok, kk):
        return _row_copy(yr_ref, dest_ref[tok * TOP_K + kk], buf_sc.at[kk], tok, sem)

    def issue(tok, carry):
        for kk in range(TOP_K):
            row_copy(tok, kk).start()
        return carry

    lax.fori_loop(0, tm, issue, 0)

    def drain(tok, carry):
        for kk in range(TOP_K):
            row_copy(tok, kk).wait()
        return carry

    lax.fori_loop(0, tm, drain, 0)

    gate = gate_ref[...]
    acc = h_ref[...]
    for kk in range(TOP_K):
        acc = acc + gate[:, kk:kk + 1] * buf_sc[kk]
    out_ref[...] = _rms(acc, nw_ref[...])


def _combine(dest_flat, h, gate, nw, y_rows):
    t = h.shape[0]
    tm = min(TM_ROWS, t)
    return pl.pallas_call(
        _combine_kernel,
        grid=(t // tm,),
        in_specs=[
            pl.BlockSpec((tm * TOP_K,), lambda i: (i,), memory_space=pltpu.SMEM),
            pl.BlockSpec((tm, D_MODEL), lambda i: (i, 0)),
            pl.BlockSpec((tm, LANES), lambda i: (i, 0)),
            pl.BlockSpec((1, D_MODEL), lambda i: (0, 0)),
            pl.BlockSpec(memory_space=pl.ANY),
        ],
        out_specs=pl.BlockSpec((tm, D_MODEL), lambda i: (i, 0)),
        out_shape=jax.ShapeDtypeStruct((t, D_MODEL), F32),
        scratch_shapes=[pltpu.VMEM((TOP_K, tm, D_MODEL), F32), pltpu.SemaphoreType.DMA],
        compiler_params=_cparams(("arbitrary",)),
        name="combine",
    )(dest_flat, h, gate, nw, y_rows)


def _pad_lanes(a, offset, axis):
    n = a.shape[axis]
    pads = [(0, 0)] * a.ndim
    pads[axis] = (offset, LANES - offset - n)
    return jnp.pad(a, pads)


def _layer(x, norm_mix_w, w_in, rg_conv_w, rg_conv_b, rg_gate_a_w, rg_gate_a_b, rg_gate_x_w,
           rg_gate_x_b, rg_lambda, gdn_conv_w, gdn_a_log, gdn_dt_bias, gdn_norm_w, w_branch_rg,
           w_branch_gdn, w_out, norm_ffn_w, router_w, router_b, moe_w_gate_up, moe_b_gate_up,
           moe_w_down, moe_b_down, out_norm_w):
    t = x.shape[0]
    row = lambda a: a.reshape(1, -1).astype(F32)

    small_lo = 2 * RG_WIDTH + 2 * KEY_DIM + 2 * VAL_DIM
    small_hi = small_lo + 2 * V_HEADS
    w_z = jnp.concatenate([w_in[:, :small_lo], w_in[:, small_hi:]], axis=1).astype(BF16)
    w_ba = _pad_lanes(w_in[:, small_lo:small_hi], 0, 1)
    w_ba_hi = w_ba.astype(BF16)
    w_ba_lo = (w_ba - w_ba_hi.astype(F32)).astype(BF16)
    z, ba, bat = _inproj(x, row(norm_mix_w), w_z, jnp.concatenate([w_ba_hi, w_ba_lo], axis=1))

    hg = _rg_branch(z, rg_conv_w, row(rg_conv_b), rg_gate_a_w.astype(BF16), row(rg_gate_a_b),
                    rg_gate_x_w.astype(BF16), row(rg_gate_x_b), row(rg_lambda))

    q, k, v = _gdn_prep(z, gdn_conv_w)
    arow = jnp.stack([_pad_lanes(gdn_a_log, V_HEADS, 0), _pad_lanes(gdn_dt_bias, V_HEADS, 0)])
    o = _gdn_core(q, k, v, ba, bat, arow, arow.T)

    rw = _pad_lanes(router_w, 0, 1)
    rb = _pad_lanes(router_b, 0, 0).reshape(1, LANES)
    h, hn, sel, gate, counts = _merge_router(
        x, hg, o, z, w_branch_rg.astype(BF16), w_branch_gdn.astype(BF16), w_out.astype(BF16),
        row(gdn_norm_w), row(norm_ffn_w), rw, rb)

    counts = counts[0, :N_EXPERTS]
    padded = ((counts + MOE_TILE - 1) // MOE_TILE) * MOE_TILE
    pends = jnp.cumsum(padded)
    pstarts = pends - padded
    n_tiles = (t * TOP_K) // MOE_TILE + N_EXPERTS
    tile_start = jnp.arange(n_tiles, dtype=I32) * MOE_TILE
    tile_valid = (tile_start < pends[-1]).astype(I32)
    tile_e = jnp.minimum(jnp.sum(pends[None, :] <= tile_start[:, None], axis=1), N_EXPERTS - 1).astype(I32)
    prev_e = jnp.concatenate([jnp.full((1,), -1, I32), tile_e[:-1]])
    tile_first = jnp.logical_and(tile_e != prev_e, tile_valid > 0).astype(I32)
    group_end = jnp.any(jnp.logical_and((tile_start + MOE_TILE)[:, None] == pends[None, :],
                                        padded[None, :] > 0), axis=1)
    zflag = jnp.logical_or(group_end, tile_valid == 0).astype(I32)

    dest = _dest_rows(sel, _pad_lanes(pstarts.astype(I32), 0, 0).reshape(1, LANES))
    dest_flat = dest[:, :TOP_K].reshape(t * TOP_K)

    x_rows = _dispatch(dest_flat, zflag, hn)
    y_rows = _experts(tile_e, tile_valid, tile_first, x_rows, moe_w_gate_up,
                      moe_b_gate_up.reshape(N_EXPERTS, 1, 2 * D_EXPERT), moe_w_down,
                      moe_b_down.reshape(N_EXPERTS, 1, D_MODEL))

    return _combine(dest_flat, h, gate, row(out_norm_w), y_rows)


def kernel(x, norm_mix_w, w_in, rg_conv_w, rg_conv_b, rg_gate_a_w, rg_gate_a_b, rg_gate_x_w,
           rg_gate_x_b, rg_lambda, gdn_conv_w, gdn_A_log, gdn_dt_bias, gdn_norm_w, w_branch_rg,
           w_branch_gdn, w_out, norm_ffn_w, router_w, router_b, moe_w_gate_up, moe_b_gate_up,
           moe_w_down, moe_b_down, norm_final_w):
    assert x.shape[0] == 1 and norm_mix_w.shape[0] == 1
    out = _layer(x[0], norm_mix_w[0], w_in[0], rg_conv_w[0], rg_conv_b[0], rg_gate_a_w[0],
                 rg_gate_a_b[0], rg_gate_x_w[0], rg_gate_x_b[0], rg_lambda[0], gdn_conv_w[0],
                 gdn_A_log[0], gdn_dt_bias[0], gdn_norm_w[0], w_branch_rg[0], w_branch_gdn[0],
                 w_out[0], norm_ffn_w[0], router_w[0], router_b[0], moe_w_gate_up[0],
                 moe_b_gate_up[0], moe_w_down[0], moe_b_down[0], norm_final_w)
    return out[None]
```

```python
import functools

import jax
import jax.numpy as jnp
from jax import lax
from jax.experimental import pallas as pl
from jax.experimental.pallas import tpu as pltpu

F32 = jnp.float32
BF16 = jnp.bfloat16
I32 = jnp.int32
HI = lax.Precision.HIGHEST

D_MODEL = 1024
RG_WIDTH = 1024
RG_BLOCKS = 4
RG_BLOCK_DIM = RG_WIDTH // RG_BLOCKS
RG_C = 8.0
CONV_K = 4
QK_HEADS = 8
V_HEADS = 16
HEAD_DIM = 128
KEY_DIM = QK_HEADS * HEAD_DIM
VAL_DIM = V_HEADS * HEAD_DIM
CHUNK = 64
N_EXPERTS = 32
TOP_K = 4
D_EXPERT = 1024
SWIGLU_ALPHA = 1.702
SWIGLU_LIMIT = 7.0
EPS = 1e-6

LANES = 128
SUBLANES = 8
VMEM_LIMIT = 56 * 1024 * 1024

Z_RGX, Z_RGY, Z_Q, Z_K = 0, 1, 2, 3
Z_V, Z_ZG = 2, 3
Z_GA, Z_GB = 8, 9
Z_WIDTH = 10 * 1024

TM_INPROJ = 1024
TN_INPROJ = 2048
TM_RG = 256
GDN_NCH = 2
TM_MERGE = 512
TM_ROWS = 256
MOE_TILE = 512


def _cparams(sem):
    return pltpu.CompilerParams(dimension_semantics=sem, vmem_limit_bytes=VMEM_LIMIT)


def _softplus(x):
    return jnp.maximum(x, 0.0) + jnp.log1p(jnp.exp(-jnp.abs(x)))


def _sigmoid(x):
    return 1.0 / (1.0 + jnp.exp(-x))


def _silu(x):
    return x * _sigmoid(x)


def _gelu_tanh(x):
    c = 0.7978845608028654
    return 0.5 * x * (1.0 + jnp.tanh(c * (x + 0.044715 * (x * x * x))))


def _rms(x, w):
    return x * lax.rsqrt(jnp.mean(x * x, axis=-1, keepdims=True) + EPS) * w


def _dot(a, b):
    return jnp.dot(a, b, preferred_element_type=F32)


def _dot_hi(a, b):
    return jnp.dot(a, b, precision=HI, preferred_element_type=F32)


def _bf16_terms(x):
    t0 = x.astype(BF16)
    r = x - t0.astype(F32)
    t1 = r.astype(BF16)
    t2 = (r - t1.astype(F32)).astype(BF16)
    return t0, t1, t2


def _dot_nt(a, b, precision=None):
    return lax.dot_general(a, b, (((1,), (1,)), ((), ())), precision=precision,
                           preferred_element_type=F32)


def _dot_tn(a, b):
    return lax.dot_general(a, b, (((0,), (0,)), ((), ())), preferred_element_type=F32)


def _inproj_kernel(x_ref, nw_ref, w_ref, wba_ref, z_ref, ba_ref, bat_ref, xn_sc):
    @pl.when(pl.program_id(1) == 0)
    def _():
        xn = _rms(x_ref[...], nw_ref[...])
        x_hi = xn.astype(BF16)
        xn_sc[...] = x_hi
        x_lo = (xn - x_hi.astype(F32)).astype(BF16)
        wba = wba_ref[...]
        hi = _dot(x_hi, wba)
        ba = hi[:, :LANES] + hi[:, LANES:] + _dot(x_lo, wba[:, :LANES])
        ba_ref[...] = ba
        bat_ref[...] = ba.T

    z_ref[...] = _dot(xn_sc[...], w_ref[...]).astype(BF16)


def _inproj(x, nw, w_z, w_ba):
    t = x.shape[0]
    tm, tn = min(TM_INPROJ, t), TN_INPROJ
    return pl.pallas_call(
        _inproj_kernel,
        grid=(t // tm, Z_WIDTH // tn),
        in_specs=[
            pl.BlockSpec((tm, D_MODEL), lambda i, j: (i, 0)),
            pl.BlockSpec((1, D_MODEL), lambda i, j: (0, 0)),
            pl.BlockSpec((D_MODEL, tn), lambda i, j: (0, j)),
            pl.BlockSpec((D_MODEL, 2 * LANES), lambda i, j: (0, 0)),
        ],
        out_specs=[
            pl.BlockSpec((tm, tn), lambda i, j: (i, j)),
            pl.BlockSpec((tm, LANES), lambda i, j: (i, 0)),
            pl.BlockSpec((LANES, tm), lambda i, j: (0, i)),
        ],
        out_shape=[
            jax.ShapeDtypeStruct((t, Z_WIDTH), BF16),
            jax.ShapeDtypeStruct((t, LANES), F32),
            jax.ShapeDtypeStruct((LANES, t), F32),
        ],
        scratch_shapes=[pltpu.VMEM((tm, D_MODEL), BF16)],
        compiler_params=_cparams(("arbitrary", "arbitrary")),
        name="inproj",
    )(x, nw, w_z, w_ba)


def _causal_conv(x_ref, stage_sc, cw_ref, lo):
    tm, width = x_ref.shape
    cols = slice(lo, lo + width)
    stage_sc[SUBLANES:, cols] = x_ref[...].astype(F32)
    off = SUBLANES - (CONV_K - 1)
    acc = cw_ref[0:1, cols] * stage_sc[off:off + tm, cols]
    for k in range(1, CONV_K):
        acc = acc + cw_ref[k:k + 1, cols] * stage_sc[off + k:off + k + tm, cols]
    stage_sc[0:SUBLANES, cols] = stage_sc[tm:tm + SUBLANES, cols]
    return acc


def _rg_kernel(x_ref, y_ref, cw_ref, cb_ref, wa_ref, ba_ref, wx_ref, bx_ref, lam_ref, out_ref,
               prev_sc, h_sc):
    i = pl.program_id(0)
    tm = x_ref.shape[0]

    @pl.when(i == 0)
    def _():
        prev_sc[0:SUBLANES, :] = jnp.zeros((SUBLANES, RG_WIDTH), F32)
        h_sc[...] = jnp.zeros_like(h_sc)

    xa = _causal_conv(x_ref, prev_sc, cw_ref, 0) + cb_ref[...]

    xab = xa.astype(BF16)
    r_parts, i_parts = [], []
    for blk in range(RG_BLOCKS):
        xs = xab[:, blk * RG_BLOCK_DIM:(blk + 1) * RG_BLOCK_DIM]
        r_parts.append(_dot(xs, wa_ref[blk]))
        i_parts.append(_dot(xs, wx_ref[blk]))
    r = _sigmoid(jnp.concatenate(r_parts, axis=1) + ba_ref[...])
    ig = _sigmoid(jnp.concatenate(i_parts, axis=1) + bx_ref[...])

    log_a = (-RG_C) * r * _softplus(-lam_ref[...])
    a = jnp.exp(log_a)
    m2 = jnp.maximum(1.0 - a * a, 0.0)
    mult = jnp.where(m2 > 0.0, m2 * lax.rsqrt(m2), 0.0)
    rows = lax.broadcasted_iota(I32, (tm, 1), 0)
    mult = jnp.where(jnp.logical_and(rows == 0, i == 0), 1.0, mult)
    b = mult * (ig * xa)

    groups = tm // SUBLANES
    a = a.reshape(groups, SUBLANES, RG_WIDTH)
    b = b.reshape(groups, SUBLANES, RG_WIDTH)
    sub = lax.broadcasted_iota(I32, (1, SUBLANES, 1), 1)
    s = 1
    while s < SUBLANES:
        keep = sub >= s
        a_sh = pltpu.roll(a, s, axis=1)
        b_sh = pltpu.roll(b, s, axis=1)
        b = b + jnp.where(keep, a * b_sh, 0.0)
        a = jnp.where(keep, a * a_sh, a)
        s *= 2
    h_prev = h_sc[...]
    hs = []
    for g in range(groups):
        hs.append(b[g] + a[g] * h_prev)
        h_prev = hs[-1][SUBLANES - 1:SUBLANES, :]
    h_sc[...] = h_prev
    h = jnp.concatenate(hs, axis=0)
    out_ref[...] = (h * _gelu_tanh(y_ref[...].astype(F32))).astype(BF16)


def _rg_branch(z, cw, cb, wa, ba, wx, bx, lam):
    t = z.shape[0]
    tm = min(TM_RG, t)
    full = lambda shape: pl.BlockSpec(shape, lambda i: (0,) * len(shape))
    return pl.pallas_call(
        _rg_kernel,
        grid=(t // tm,),
        in_specs=[
            pl.BlockSpec((tm, RG_WIDTH), lambda i: (i, Z_RGX)),
            pl.BlockSpec((tm, RG_WIDTH), lambda i: (i, Z_RGY)),
            full((CONV_K, RG_WIDTH)),
            full((1, RG_WIDTH)),
            full((RG_BLOCKS, RG_BLOCK_DIM, RG_BLOCK_DIM)),
            full((1, RG_WIDTH)),
            full((RG_BLOCKS, RG_BLOCK_DIM, RG_BLOCK_DIM)),
            full((1, RG_WIDTH)),
            full((1, RG_WIDTH)),
        ],
        out_specs=pl.BlockSpec((tm, RG_WIDTH), lambda i: (i, 0)),
        out_shape=jax.ShapeDtypeStruct((t, RG_WIDTH), BF16),
        scratch_shapes=[pltpu.VMEM((SUBLANES + tm, RG_WIDTH), F32), pltpu.VMEM((1, RG_WIDTH), F32)],
        compiler_params=_cparams(("arbitrary",)),
        name="rg_branch",
    )(z, z, cw, cb, wa, ba, wx, bx, lam)


def _l2norm_heads(x, n_heads):
    parts = []
    for h in range(n_heads):
        xh = x[:, h * HEAD_DIM:(h + 1) * HEAD_DIM]
        parts.append(xh * lax.rsqrt(jnp.sum(xh * xh, axis=-1, keepdims=True) + EPS))
    return jnp.concatenate(parts, axis=1)


def _conv_silu(x_ref, prev_sc, cw_ref, lo):
    return _silu(_causal_conv(x_ref, prev_sc, cw_ref, lo))


def _unit_lower_inverses(mats):
    c = mats[0].shape[0]
    ri = lax.broadcasted_iota(I32, (c, c), 0)
    ci = lax.broadcasted_iota(I32, (c, c), 1)
    eye = jnp.where(ri == ci, 1.0, 0.0)
    ps = [eye - a for a in mats]
    xbs = [a.astype(BF16) for a in mats]
    xbs = [_dot(xb, xb).astype(BF16) for xb in xbs]
    power = 2
    while power < c:
        ps = [p + _dot(p.astype(BF16), xb) for p, xb in zip(ps, xbs)]
        power *= 2
        if power < c:
            xbs = [_dot(xb, xb).astype(BF16) for xb in xbs]
    return ps


def _gdn_kernel(q_ref, k_ref, v_ref, cw_ref, ba_ref, bat_ref, arow_ref, acol_ref, msum_ref,
                o_ref, s_sc, gt_sc, prev_sc, *, nch):
    hg_n = V_HEADS
    c = CHUNK
    tile = nch * c
    scale = HEAD_DIM ** -0.5

    @pl.when(pl.program_id(0) == 0)
    def _():
        s_sc[...] = jnp.zeros_like(s_sc)
        prev_sc[0:SUBLANES, :] = jnp.zeros((SUBLANES, 2 * KEY_DIM + VAL_DIM), F32)

    q_all = _l2norm_heads(_conv_silu(q_ref, prev_sc, cw_ref, 0), QK_HEADS)
    k_all = _l2norm_heads(_conv_silu(k_ref, prev_sc, cw_ref, KEY_DIM), QK_HEADS)
    v_all = _conv_silu(v_ref, prev_sc, cw_ref, 2 * KEY_DIM)
    q_bf, k_bf = q_all.astype(BF16), k_all.astype(BF16)

    ba = ba_ref[...]
    arow = arow_ref[...]
    g_all = -jnp.exp(arow[0:1, :]) * _softplus(ba + arow[1:2, :])
    beta_all = _sigmoid(ba)
    msum = msum_ref[...]
    sums = sum(_dot(msum, term) for term in _bf16_terms(g_all))
    gcs = sums[:tile, :]
    gls = sums[tile:, :]
    egc = jnp.exp(gcs)
    ekd = jnp.exp(gls - gcs)
    acol = acol_ref[...]
    g_t = -jnp.exp(acol[:, 0:1]) * _softplus(bat_ref[...] + acol[:, 1:2])
    gt_sc[...] = sum(_dot_nt(term, msum[:tile, :]) for term in _bf16_terms(g_t))

    lane = lax.broadcasted_iota(I32, (tile, LANES), 1)
    ri = lax.broadcasted_iota(I32, (c, c), 0)
    ci = lax.broadcasted_iota(I32, (c, c), 1)
    causal = ri >= ci
    strict = ri > ci

    def column(arr, lane_idx):
        return jnp.sum(jnp.where(lane == lane_idx, arr, 0.0), axis=1, keepdims=True)

    cols = []
    for j in range(hg_n):
        head = j
        b_col = column(beta_all, head)
        egc_col = column(egc, V_HEADS + head)
        cols.append(dict(
            b=b_col, gcs=column(gcs, V_HEADS + head), egc=egc_col,
            ekd=column(ekd, V_HEADS + head), be=b_col * egc_col,
            g_row=gt_sc[pl.ds(V_HEADS + head, 1), :]))

    pairs = [(n, j) for n in range(nch) for j in range(hg_n)]
    rows_of = lambda n: slice(n * c, (n + 1) * c)
    head_cols = lambda j: slice(j * HEAD_DIM, (j + 1) * HEAD_DIM)
    kq = {}
    for n in range(nch):
        for qh in range(hg_n // 2):
            k = k_bf[rows_of(n), head_cols(qh)]
            q = q_bf[rows_of(n), head_cols(qh)]
            kq[(n, qh)] = dict(kk=_dot_nt(k, k), qk=_dot_nt(q, k), kf=k_all[rows_of(n), head_cols(qh)],
                               qf=q_all[rows_of(n), head_cols(qh)])
    decays, a_mats = {}, []
    for n, j in pairs:
        col = cols[j]
        diff = col["gcs"][rows_of(n), :] - col["g_row"][:, rows_of(n)]
        decay = jnp.where(causal, jnp.exp(jnp.where(causal, diff, 0.0)), 0.0)
        decays[(n, j)] = decay
        a_mats.append(jnp.where(strict, col["b"][rows_of(n), :] * kq[(n, j // 2)]["kk"] * decay, 0.0))
    t_mats = _unit_lower_inverses(a_mats)
    uws, attns, q_decs, k_decs = {}, {}, {}, {}
    for (n, j), t_mat in zip(pairs, t_mats):
        col, kqn = cols[j], kq[(n, j // 2)]
        v = v_all[rows_of(n), head_cols(j)]
        rhs = jnp.concatenate([v * col["b"][rows_of(n), :], kqn["kf"] * col["be"][rows_of(n), :]], axis=1)
        uws[(n, j)] = _dot(t_mat.astype(BF16), rhs.astype(BF16))
        attns[(n, j)] = (kqn["qk"] * decays[(n, j)]).astype(BF16)
        q_decs[(n, j)] = kqn["qf"] * col["egc"][rows_of(n), :]
        k_decs[(n, j)] = (kqn["kf"] * col["ekd"][rows_of(n), :]).astype(BF16)

    states = [s_sc[j] for j in range(hg_n)]
    for n in range(nch):
        heads = range(hg_n)
        ws_qs = [_dot(jnp.concatenate([uws[(n, j)][:, HEAD_DIM:], q_decs[(n, j)]], axis=0).astype(BF16),
                      states[j].astype(BF16)) for j in heads]
        v_news = [(uws[(n, j)][:, :HEAD_DIM] - ws_qs[j][:c, :]).astype(BF16) for j in heads]
        outs = [(ws_qs[j][c:, :] + _dot(attns[(n, j)], v_news[j])) * scale for j in heads]
        for j in heads:
            o_ref[rows_of(n), head_cols(j)] = outs[j].astype(BF16)
        egls = [jnp.exp(cols[j]["g_row"][:, (n + 1) * c - 1:(n + 1) * c]) for j in heads]
        states = [states[j] * egls[j] + _dot_tn(k_decs[(n, j)], v_news[j]) for j in heads]
    for j in range(hg_n):
        s_sc[j] = states[j]


def _gdn_branch(z, cw, ba, bat, arow, acol):
    t = z.shape[0]
    nch = GDN_NCH
    tile = nch * CHUNK
    idx = jnp.arange(tile)
    same = (idx[:, None] // CHUNK) == (idx[None, :] // CHUNK)
    mtri = jnp.logical_and(same, idx[:, None] >= idx[None, :])
    msum = jnp.concatenate([mtri, same], axis=0).astype(BF16)
    return pl.pallas_call(
        functools.partial(_gdn_kernel, nch=nch),
        grid=(t // tile,),
        in_specs=[
            pl.BlockSpec((tile, KEY_DIM), lambda n: (n, Z_Q)),
            pl.BlockSpec((tile, KEY_DIM), lambda n: (n, Z_K)),
            pl.BlockSpec((tile, VAL_DIM), lambda n: (n, Z_V)),
            pl.BlockSpec((CONV_K, 2 * KEY_DIM + VAL_DIM), lambda n: (0, 0)),
            pl.BlockSpec((tile, LANES), lambda n: (n, 0)),
            pl.BlockSpec((LANES, tile), lambda n: (0, n)),
            pl.BlockSpec((2, LANES), lambda n: (0, 0)),
            pl.BlockSpec((LANES, 2), lambda n: (0, 0)),
            pl.BlockSpec((2 * tile, tile), lambda n: (0, 0)),
        ],
        out_specs=pl.BlockSpec((tile, VAL_DIM), lambda n: (n, 0)),
        out_shape=jax.ShapeDtypeStruct((t, VAL_DIM), BF16),
        scratch_shapes=[pltpu.VMEM((V_HEADS, HEAD_DIM, HEAD_DIM), F32), pltpu.VMEM((LANES, tile), F32),
                        pltpu.VMEM((SUBLANES + tile, 2 * KEY_DIM + VAL_DIM), F32)],
        compiler_params=_cparams(("arbitrary",)),
        name="gdn_branch",
    )(z, z, z, cw, ba, bat, arow, acol, msum)


def _merge_kernel(x_ref, hg_ref, o_ref, zg_ref, ga_ref, gb_ref, wrg_ref, wgdn_ref, wout_ref,
                  gnw_ref, fnw_ref, rw_ref, rb_ref, lstrict_ref,
                  h_ref, hn_ref, sel_ref, gate_ref, cnt_ref, carry_sc):
    tm = x_ref.shape[0]

    @pl.when(pl.program_id(0) == 0)
    def _():
        carry_sc[...] = jnp.zeros_like(carry_sc)

    ya = _dot(hg_ref[...], wrg_ref[...])
    o = o_ref[...].astype(F32)
    zg = zg_ref[...].astype(F32)
    gnw = gnw_ref[...]
    parts = []
    for h in range(V_HEADS):
        sl = slice(h * HEAD_DIM, (h + 1) * HEAD_DIM)
        parts.append((_rms(o[:, sl], gnw) * _silu(zg[:, sl])).astype(BF16))
    yb = _dot(jnp.concatenate(parts, axis=1), wgdn_ref[...])
    mix = _sigmoid(ga_ref[...].astype(F32)) * ya + _sigmoid(gb_ref[...].astype(F32)) * yb
    h = x_ref[...] + _dot(mix.astype(BF16), wout_ref[...])
    h_ref[...] = h

    hn = _rms(h, fnw_ref[...])
    hn_ref[...] = hn
    lane = lax.broadcasted_iota(I32, (tm, LANES), 1)
    lane_f = lane.astype(F32)
    neg_inf = float("-inf")
    hn_hi = hn.astype(BF16)
    hn_lo = (hn - hn_hi.astype(F32)).astype(BF16)
    rw = rw_ref[...]
    hi = _dot(hn_hi, rw)
    raw = hi[:, :LANES] + hi[:, LANES:] + _dot(hn_lo, rw[:, :LANES])
    logits = jnp.where(lane < N_EXPERTS, raw + rb_ref[...], neg_inf)
    vals, idxs = [], []
    multi_hot = jnp.zeros((tm, LANES), F32)
    for _ in range(TOP_K):
        m = jnp.max(logits, axis=-1, keepdims=True)
        first = jnp.min(jnp.where(logits == m, lane_f, float(LANES)), axis=-1, keepdims=True)
        hit = lane_f == first
        multi_hot = multi_hot + jnp.where(hit, 1.0, 0.0)
        logits = jnp.where(hit, neg_inf, logits)
        vals.append(m)
        idxs.append(first)
    exps = [jnp.exp(v - vals[0]) for v in vals]
    denom = exps[0] + exps[1] + exps[2] + exps[3]

    before = _dot(lstrict_ref[...], multi_hot.astype(BF16)) + carry_sc[...]
    carry_sc[...] = carry_sc[...] + jnp.sum(multi_hot, axis=0, keepdims=True)
    cnt_ref[...] = carry_sc[...].astype(I32)

    sel = jnp.zeros((tm, LANES), F32)
    gate = jnp.zeros((tm, LANES), F32)
    for kk in range(TOP_K):
        rank = jnp.sum(jnp.where(lane_f == idxs[kk], before, 0.0), axis=-1, keepdims=True)
        sel = jnp.where(lane == kk, idxs[kk], sel)
        sel = jnp.where(lane == TOP_K + kk, rank, sel)
        gate = jnp.where(lane == kk, exps[kk] / denom, gate)
    sel_ref[...] = sel.astype(I32)
    gate_ref[...] = gate


def _merge_router(x, hg, o, z, wrg, wgdn, wout, gnw, fnw, rw, rb):
    t = x.shape[0]
    tm = min(TM_MERGE, t)
    idx = jnp.arange(tm)
    lstrict = (idx[:, None] > idx[None, :]).astype(BF16)
    full = lambda shape: pl.BlockSpec(shape, lambda i: (0,) * len(shape))
    row = lambda width, col=0: pl.BlockSpec((tm, width), lambda i: (i, col))
    return pl.pallas_call(
        _merge_kernel,
        grid=(t // tm,),
        in_specs=[
            row(D_MODEL), row(RG_WIDTH), row(VAL_DIM), row(VAL_DIM, Z_ZG), row(D_MODEL, Z_GA),
            row(D_MODEL, Z_GB),
            full((RG_WIDTH, D_MODEL)), full((VAL_DIM, D_MODEL)), full((D_MODEL, D_MODEL)),
            full((1, HEAD_DIM)), full((1, D_MODEL)), full((D_MODEL, 2 * LANES)), full((1, LANES)),
            full((tm, tm)),
        ],
        out_specs=[row(D_MODEL), row(D_MODEL), row(LANES), row(LANES), full((1, LANES))],
        out_shape=[
            jax.ShapeDtypeStruct((t, D_MODEL), F32),
            jax.ShapeDtypeStruct((t, D_MODEL), F32),
            jax.ShapeDtypeStruct((t, LANES), I32),
            jax.ShapeDtypeStruct((t, LANES), F32),
            jax.ShapeDtypeStruct((1, LANES), I32),
        ],
        scratch_shapes=[pltpu.VMEM((1, LANES), F32)],
        compiler_params=_cparams(("arbitrary",)),
        name="merge_router",
    )(x, hg, o, z, z, z, wrg, wgdn, wout, gnw, fnw, rw, rb, lstrict)


def _dest_kernel(sel_ref, pstart_ref, dest_ref):
    tm = sel_ref.shape[0]
    sel = sel_ref[...].astype(F32)
    lane = lax.broadcasted_iota(I32, (tm, LANES), 1)
    lane_f = lane.astype(F32)
    pstart = pstart_ref[...].astype(F32)
    dest = jnp.zeros((tm, LANES), F32)
    for kk in range(TOP_K):
        e = jnp.sum(jnp.where(lane == kk, sel, 0.0), axis=-1, keepdims=True)
        rank = jnp.sum(jnp.where(lane == TOP_K + kk, sel, 0.0), axis=-1, keepdims=True)
        base = jnp.sum(jnp.where(lane_f == e, pstart, 0.0), axis=-1, keepdims=True)
        dest = jnp.where(lane == kk, base + rank, dest)
    dest_ref[...] = dest.astype(I32)


def _dest_rows(sel, pstart):
    t = sel.shape[0]
    tm = min(1024, t)
    return pl.pallas_call(
        _dest_kernel,
        grid=(t // tm,),
        in_specs=[pl.BlockSpec((tm, LANES), lambda i: (i, 0)), pl.BlockSpec((1, LANES), lambda i: (0, 0))],
        out_specs=pl.BlockSpec((tm, LANES), lambda i: (i, 0)),
        out_shape=jax.ShapeDtypeStruct((t, LANES), I32),
        compiler_params=_cparams(("arbitrary",)),
        name="dest_rows",
    )(sel, pstart)


def _row_copy(src_ref, src_row, dst_ref, dst_row, sem):
    return pltpu.make_async_copy(src_ref.at[pl.ds(src_row, 1), :], dst_ref.at[pl.ds(dst_row, 1), :], sem)


def _dispatch_kernel(dest_ref, zflag_ref, hn_ref, xr_ref, zero_sc, sem, zsem):
    tm = hn_ref.shape[0]
    n_tiles = zflag_ref.shape[0]

    def zero_copy(tile):
        start = pl.multiple_of(tile * MOE_TILE, MOE_TILE)
        return pltpu.make_async_copy(zero_sc, xr_ref.at[pl.ds(start, MOE_TILE), :], zsem)

    @pl.when(pl.program_id(0) == 0)
    def _():
        zero_sc[...] = jnp.zeros_like(zero_sc)

        def start(tile, carry):
            @pl.when(zflag_ref[tile] > 0)
            def _():
                zero_copy(tile).start()
            return carry

        def wait(tile, carry):
            @pl.when(zflag_ref[tile] > 0)
            def _():
                zero_copy(tile).wait()
            return carry

        lax.fori_loop(0, n_tiles, start, 0)
        lax.fori_loop(0, n_tiles, wait, 0)

    def issue(tok, carry):
        for kk in range(TOP_K):
            _row_copy(hn_ref, tok, xr_ref, dest_ref[tok * TOP_K + kk], sem).start(priority=kk % 2)
        return carry

    lax.fori_loop(0, tm, issue, 0)

    def drain(tok, carry):
        for kk in range(TOP_K):
            _row_copy(hn_ref, tok, xr_ref, dest_ref[tok * TOP_K + kk], sem).wait()
        return carry

    lax.fori_loop(0, tm, drain, 0)


def _dispatch(dest_flat, zflag, hn):
    t = hn.shape[0]
    tm = min(TM_ROWS, t)
    n_tiles = zflag.shape[0]
    return pl.pallas_call(
        _dispatch_kernel,
        grid=(t // tm,),
        in_specs=[
            pl.BlockSpec((tm * TOP_K,), lambda i: (i,), memory_space=pltpu.SMEM),
            pl.BlockSpec((n_tiles,), lambda i: (0,), memory_space=pltpu.SMEM),
            pl.BlockSpec((tm, D_MODEL), lambda i: (i, 0)),
        ],
        out_specs=pl.BlockSpec(memory_space=pl.ANY),
        out_shape=jax.ShapeDtypeStruct((n_tiles * MOE_TILE, D_MODEL), F32),
        scratch_shapes=[pltpu.VMEM((MOE_TILE, D_MODEL), F32), pltpu.SemaphoreType.DMA,
                        pltpu.SemaphoreType.DMA],
        compiler_params=_cparams(("arbitrary",)),
        name="dispatch",
    )(dest_flat, zflag, hn)


def _expert_kernel(te_ref, tv_ref, tf_ref, x_ref, wgu_ref, bgu_ref, wd_ref, bd_ref, y_ref,
                   wgu_sc, wd_sc):
    i = pl.program_id(0)

    @pl.when(tf_ref[i] > 0)
    def _():
        wgu_sc[...] = wgu_ref[0].astype(BF16)
        wd_sc[...] = wd_ref[0].astype(BF16)

    @pl.when(tv_ref[i] > 0)
    def _():
        gu = _dot(x_ref[...].astype(BF16), wgu_sc[...]) + bgu_ref[0]
        gate = jnp.minimum(gu[:, :D_EXPERT], SWIGLU_LIMIT)
        up = jnp.clip(gu[:, D_EXPERT:], -SWIGLU_LIMIT, SWIGLU_LIMIT)
        hid = (up + 1.0) * (gate * _sigmoid(SWIGLU_ALPHA * gate))
        y_ref[...] = _dot(hid.astype(BF16), wd_sc[...]) + bd_ref[0]

    @pl.when(tv_ref[i] == 0)
    def _():
        y_ref[...] = jnp.zeros_like(y_ref)


def _experts(tile_e, tile_valid, tile_first, x_rows, wgu, bgu, wd, bd):
    n_rows = x_rows.shape[0]
    n_tiles = n_rows // MOE_TILE
    grid_spec = pltpu.PrefetchScalarGridSpec(
        num_scalar_prefetch=3,
        grid=(n_tiles,),
        in_specs=[
            pl.BlockSpec((MOE_TILE, D_MODEL), lambda i, te, tv, tf: (i, 0)),
            pl.BlockSpec((1, D_MODEL, 2 * D_EXPERT), lambda i, te, tv, tf: (te[i], 0, 0)),
            pl.BlockSpec((1, 1, 2 * D_EXPERT), lambda i, te, tv, tf: (te[i], 0, 0)),
            pl.BlockSpec((1, D_EXPERT, D_MODEL), lambda i, te, tv, tf: (te[i], 0, 0)),
            pl.BlockSpec((1, 1, D_MODEL), lambda i, te, tv, tf: (te[i], 0, 0)),
        ],
        out_specs=pl.BlockSpec((MOE_TILE, D_MODEL), lambda i, te, tv, tf: (i, 0)),
        scratch_shapes=[pltpu.VMEM((D_MODEL, 2 * D_EXPERT), BF16), pltpu.VMEM((D_EXPERT, D_MODEL), BF16)],
    )
    return pl.pallas_call(
        _expert_kernel,
        grid_spec=grid_spec,
        out_shape=jax.ShapeDtypeStruct((n_rows, D_MODEL), F32),
        compiler_params=_cparams(("arbitrary",)),
        name="experts",
    )(tile_e, tile_valid, tile_first, x_rows, wgu, bgu, wd, bd)


def _combine_kernel(dest_ref, h_ref, gate_ref, nw_ref, yr_ref, out_ref, buf_sc, sem):
    tm = h_ref.shape[0]

    def row_copy(tok, kk):
        return _row_copy(yr_ref, dest_ref[tok * TOP_K + kk], buf_sc.at[kk], tok, sem)

    def issue(tok, carry):
        for kk in range(TOP_K):
            row_copy(tok, kk).start(priority=kk % 2)
        return carry

    lax.fori_loop(0, tm, issue, 0)

    def drain(tok, carry):
        for kk in range(TOP_K):
            row_copy(tok, kk).wait()
        return carry

    lax.fori_loop(0, tm, drain, 0)

    gate = gate_ref[...]
    acc = h_ref[...]
    for kk in range(TOP_K):
        acc = acc + gate[:, kk:kk + 1] * buf_sc[kk]
    out_ref[...] = _rms(acc, nw_ref[...])


def _combine(dest_flat, h, gate, nw, y_rows):
    t = h.shape[0]
    tm = min(TM_ROWS, t)
    return pl.pallas_call(
        _combine_kernel,
        grid=(t // tm,),
        in_specs=[
            pl.BlockSpec((tm * TOP_K,), lambda i: (i,), memory_space=pltpu.SMEM),
            pl.BlockSpec((tm, D_MODEL), lambda i: (i, 0)),
            pl.BlockSpec((tm, LANES), lambda i: (i, 0)),
            pl.BlockSpec((1, D_MODEL), lambda i: (0, 0)),
            pl.BlockSpec(memory_space=pl.ANY),
        ],
        out_specs=pl.BlockSpec((tm, D_MODEL), lambda i: (i, 0)),
        out_shape=jax.ShapeDtypeStruct((t, D_MODEL), F32),
        scratch_shapes=[pltpu.VMEM((TOP_K, tm, D_MODEL), F32), pltpu.SemaphoreType.DMA],
        compiler_params=_cparams(("arbitrary",)),
        name="combine",
    )(dest_flat, h, gate, nw, y_rows)


def _pad_lanes(a, offset, axis):
    n = a.shape[axis]
    pads = [(0, 0)] * a.ndim
    pads[axis] = (offset, LANES - offset - n)
    return jnp.pad(a, pads)


def _layer(x, norm_mix_w, w_in, rg_conv_w, rg_conv_b, rg_gate_a_w, rg_gate_a_b, rg_gate_x_w,
           rg_gate_x_b, rg_lambda, gdn_conv_w, gdn_a_log, gdn_dt_bias, gdn_norm_w, w_branch_rg,
           w_branch_gdn, w_out, norm_ffn_w, router_w, router_b, moe_w_gate_up, moe_b_gate_up,
           moe_w_down, moe_b_down, out_norm_w):
    t = x.shape[0]
    row = lambda a: a.reshape(1, -1).astype(F32)

    small_lo = 2 * RG_WIDTH + 2 * KEY_DIM + 2 * VAL_DIM
    small_hi = small_lo + 2 * V_HEADS
    w_z = jnp.concatenate([w_in[:, :small_lo], w_in[:, small_hi:]], axis=1).astype(BF16)
    w_ba = _pad_lanes(w_in[:, small_lo:small_hi], 0, 1)
    w_ba_hi = w_ba.astype(BF16)
    w_ba_lo = (w_ba - w_ba_hi.astype(F32)).astype(BF16)
    z, ba, bat = _inproj(x, row(norm_mix_w), w_z, jnp.concatenate([w_ba_hi, w_ba_lo], axis=1))

    hg = _rg_branch(z, rg_conv_w, row(rg_conv_b), rg_gate_a_w.astype(BF16), row(rg_gate_a_b),
                    rg_gate_x_w.astype(BF16), row(rg_gate_x_b), row(rg_lambda))

    arow = jnp.stack([_pad_lanes(gdn_a_log, V_HEADS, 0), _pad_lanes(gdn_dt_bias, V_HEADS, 0)])
    o = _gdn_branch(z, gdn_conv_w, ba, bat, arow, arow.T)

    rw = _pad_lanes(router_w, 0, 1)
    rw_hi = rw.astype(BF16)
    rw = jnp.concatenate([rw_hi, (rw - rw_hi.astype(F32)).astype(BF16)], axis=1)
    rb = _pad_lanes(router_b, 0, 0).reshape(1, LANES)
    h, hn, sel, gate, counts = _merge_router(
        x, hg, o, z, w_branch_rg.astype(BF16), w_branch_gdn.astype(BF16), w_out.astype(BF16),
        row(gdn_norm_w), row(norm_ffn_w), rw, rb)

    counts = counts[0, :N_EXPERTS]
    padded = ((counts + MOE_TILE - 1) // MOE_TILE) * MOE_TILE
    pends = jnp.cumsum(padded)
    pstarts = pends - padded
    n_tiles = (t * TOP_K) // MOE_TILE + N_EXPERTS
    tile_start = jnp.arange(n_tiles, dtype=I32) * MOE_TILE
    tile_valid = (tile_start < pends[-1]).astype(I32)
    tile_e = jnp.minimum(jnp.sum(pends[None, :] <= tile_start[:, None], axis=1), N_EXPERTS - 1).astype(I32)
    prev_e = jnp.concatenate([jnp.full((1,), -1, I32), tile_e[:-1]])
    tile_first = jnp.logical_and(tile_e != prev_e, tile_valid > 0).astype(I32)
    group_end = jnp.any(jnp.logical_and((tile_start + MOE_TILE)[:, None] == pends[None, :],
                                        padded[None, :] > 0), axis=1)
    zflag = jnp.logical_or(group_end, tile_valid == 0).astype(I32)

    dest = _dest_rows(sel, _pad_lanes(pstarts.astype(I32), 0, 0).reshape(1, LANES))
    dest_flat = dest[:, :TOP_K].reshape(t * TOP_K)

    x_rows = _dispatch(dest_flat, zflag, hn)
    y_rows = _experts(tile_e, tile_valid, tile_first, x_rows, moe_w_gate_up,
                      moe_b_gate_up.reshape(N_EXPERTS, 1, 2 * D_EXPERT), moe_w_down,
                      moe_b_down.reshape(N_EXPERTS, 1, D_MODEL))

    return _combine(dest_flat, h, gate, row(out_norm_w), y_rows)


def kernel(x, norm_mix_w, w_in, rg_conv_w, rg_conv_b, rg_gate_a_w, rg_gate_a_b, rg_gate_x_w,
           rg_gate_x_b, rg_lambda, gdn_conv_w, gdn_A_log, gdn_dt_bias, gdn_norm_w, w_branch_rg,
           w_branch_gdn, w_out, norm_ffn_w, router_w, router_b, moe_w_gate_up, moe_b_gate_up,
           moe_w_down, moe_b_down, norm_final_w):
    assert x.shape[0] == 1 and norm_mix_w.shape[0] == 1
    out = _layer(x[0], norm_mix_w[0], w_in[0], rg_conv_w[0], rg_conv_b[0], rg_gate_a_w[0],
                 rg_gate_a_b[0], rg_gate_x_w[0], rg_gate_x_b[0], rg_lambda[0], gdn_conv_w[0],
                 gdn_A_log[0], gdn_dt_bias[0], gdn_norm_w[0], w_branch_rg[0], w_branch_gdn[0],
                 w_out[0], norm_ffn_w[0], router_w[0], router_b[0], moe_w_gate_up[0],
                 moe_b_gate_up[0], moe_w_down[0], moe_b_down[0], norm_final_w)
    return out[None]
```

```python
import functools

import jax
import jax.numpy as jnp
from jax import lax
from jax.experimental import pallas as pl
from jax.experimental.pallas import tpu as pltpu

F32 = jnp.float32
BF16 = jnp.bfloat16
I32 = jnp.int32
HI = lax.Precision.HIGHEST

D_MODEL = 1024
RG_WIDTH = 1024
RG_BLOCKS = 4
RG_BLOCK_DIM = RG_WIDTH // RG_BLOCKS
RG_C = 8.0
CONV_K = 4
QK_HEADS = 8
V_HEADS = 16
HEAD_DIM = 128
KEY_DIM = QK_HEADS * HEAD_DIM
VAL_DIM = V_HEADS * HEAD_DIM
CHUNK = 64
N_EXPERTS = 32
TOP_K = 4
D_EXPERT = 1024
SWIGLU_ALPHA = 1.702
SWIGLU_LIMIT = 7.0
EPS = 1e-6

LANES = 128
SUBLANES = 8
VMEM_LIMIT = 56 * 1024 * 1024

Z_RGX, Z_RGY, Z_Q, Z_K = 0, 1, 2, 3
Z_V, Z_ZG = 2, 3
Z_GA, Z_GB = 8, 9
Z_WIDTH = 10 * 1024

TM_INPROJ = 1024
TN_INPROJ = 2048
TM_RG = 256
GDN_NCH = 2
TM_MERGE = 512
TM_ROWS = 512
MOE_TILE = 512


def _cparams(sem):
    return pltpu.CompilerParams(dimension_semantics=sem, vmem_limit_bytes=VMEM_LIMIT)


def _softplus(x):
    return jnp.maximum(x, 0.0) + jnp.log1p(jnp.exp(-jnp.abs(x)))


def _sigmoid(x):
    return 1.0 / (1.0 + jnp.exp(-x))


def _silu(x):
    return x * _sigmoid(x)


def _gelu_tanh(x):
    c = 0.7978845608028654
    return 0.5 * x * (1.0 + jnp.tanh(c * (x + 0.044715 * (x * x * x))))


def _rms(x, w):
    return x * lax.rsqrt(jnp.mean(x * x, axis=-1, keepdims=True) + EPS) * w


def _dot(a, b):
    return jnp.dot(a, b, preferred_element_type=F32)


def _dot_hi(a, b):
    return jnp.dot(a, b, precision=HI, preferred_element_type=F32)


def _bf16_terms(x):
    t0 = x.astype(BF16)
    r = x - t0.astype(F32)
    t1 = r.astype(BF16)
    t2 = (r - t1.astype(F32)).astype(BF16)
    return t0, t1, t2


def _dot_nt(a, b, precision=None):
    return lax.dot_general(a, b, (((1,), (1,)), ((), ())), precision=precision,
                           preferred_element_type=F32)


def _dot_tn(a, b):
    return lax.dot_general(a, b, (((0,), (0,)), ((), ())), preferred_element_type=F32)


def _inproj_kernel(x_ref, nw_ref, w_ref, wba_ref, z_ref, ba_ref, bat_ref, xn_sc):
    @pl.when(pl.program_id(1) == 0)
    def _():
        xn = _rms(x_ref[...], nw_ref[...])
        x_hi = xn.astype(BF16)
        xn_sc[...] = x_hi
        x_lo = (xn - x_hi.astype(F32)).astype(BF16)
        wba = wba_ref[...]
        hi = _dot(x_hi, wba)
        ba = hi[:, :LANES] + hi[:, LANES:] + _dot(x_lo, wba[:, :LANES])
        ba_ref[...] = ba
        bat_ref[...] = ba.T

    z_ref[...] = _dot(xn_sc[...], w_ref[...]).astype(BF16)


def _inproj(x, nw, w_z, w_ba):
    t = x.shape[0]
    tm, tn = min(TM_INPROJ, t), TN_INPROJ
    return pl.pallas_call(
        _inproj_kernel,
        grid=(t // tm, Z_WIDTH // tn),
        in_specs=[
            pl.BlockSpec((tm, D_MODEL), lambda i, j: (i, 0)),
            pl.BlockSpec((1, D_MODEL), lambda i, j: (0, 0)),
            pl.BlockSpec((D_MODEL, tn), lambda i, j: (0, j)),
            pl.BlockSpec((D_MODEL, 2 * LANES), lambda i, j: (0, 0)),
        ],
        out_specs=[
            pl.BlockSpec((tm, tn), lambda i, j: (i, j)),
            pl.BlockSpec((tm, LANES), lambda i, j: (i, 0)),
            pl.BlockSpec((LANES, tm), lambda i, j: (0, i)),
        ],
        out_shape=[
            jax.ShapeDtypeStruct((t, Z_WIDTH), BF16),
            jax.ShapeDtypeStruct((t, LANES), F32),
            jax.ShapeDtypeStruct((LANES, t), F32),
        ],
        scratch_shapes=[pltpu.VMEM((tm, D_MODEL), BF16)],
        compiler_params=_cparams(("arbitrary", "arbitrary")),
        name="inproj",
    )(x, nw, w_z, w_ba)


def _delay_matrix(tm):
    t = jnp.arange(tm)
    return jnp.concatenate([(t[None, :] == t[:, None] - s) for s in range(1, CONV_K)], axis=0).astype(BF16)


def _causal_conv(x_ref, hist_sc, cw_ref, delay_ref, lo):
    tm, width = x_ref.shape
    cols = slice(lo, lo + width)
    xb = x_ref[...]
    x = xb.astype(F32)
    delayed = _dot(delay_ref[...], xb)
    acc = cw_ref[CONV_K - 1:CONV_K, cols] * x
    prev = hist_sc[:, cols]
    row = lax.broadcasted_iota(I32, (SUBLANES, 1), 0)
    head = jnp.zeros((SUBLANES, width), F32)
    for s in range(1, CONV_K):
        tap = cw_ref[CONV_K - 1 - s:CONV_K - s, cols]
        acc = acc + tap * delayed[(s - 1) * tm:s * tm, :]
        head = head + tap * jnp.where(row < s, pltpu.roll(prev, s, axis=0), 0.0)
    hist_sc[:, cols] = x[tm - SUBLANES:, :]
    return jnp.concatenate([acc[:SUBLANES, :] + head, acc[SUBLANES:, :]], axis=0)


def _rg_kernel(x_ref, y_ref, cw_ref, cb_ref, wa_ref, ba_ref, wx_ref, bx_ref, lam_ref, delay_ref,
               out_ref, prev_sc, h_sc):
    i = pl.program_id(0)
    tm = x_ref.shape[0]

    @pl.when(i == 0)
    def _():
        prev_sc[...] = jnp.zeros_like(prev_sc)
        h_sc[...] = jnp.zeros_like(h_sc)

    xa = _causal_conv(x_ref, prev_sc, cw_ref, delay_ref, 0) + cb_ref[...]

    xab = xa.astype(BF16)
    r_parts, i_parts = [], []
    for blk in range(RG_BLOCKS):
        xs = xab[:, blk * RG_BLOCK_DIM:(blk + 1) * RG_BLOCK_DIM]
        r_parts.append(_dot(xs, wa_ref[blk]))
        i_parts.append(_dot(xs, wx_ref[blk]))
    r = _sigmoid(jnp.concatenate(r_parts, axis=1) + ba_ref[...])
    ig = _sigmoid(jnp.concatenate(i_parts, axis=1) + bx_ref[...])

    log_a = (-RG_C) * r * _softplus(-lam_ref[...])
    a = jnp.exp(log_a)
    m2 = jnp.maximum(1.0 - a * a, 0.0)
    mult = jnp.where(m2 > 0.0, m2 * lax.rsqrt(m2), 0.0)
    rows = lax.broadcasted_iota(I32, (tm, 1), 0)
    mult = jnp.where(jnp.logical_and(rows == 0, i == 0), 1.0, mult)
    b = mult * (ig * xa)

    groups = tm // SUBLANES
    a = a.reshape(groups, SUBLANES, RG_WIDTH)
    b = b.reshape(groups, SUBLANES, RG_WIDTH)
    sub = lax.broadcasted_iota(I32, (1, SUBLANES, 1), 1)
    s = 1
    while s < SUBLANES:
        keep = sub >= s
        a_sh = pltpu.roll(a, s, axis=1)
        b_sh = pltpu.roll(b, s, axis=1)
        b = b + jnp.where(keep, a * b_sh, 0.0)
        a = jnp.where(keep, a * a_sh, a)
        s *= 2
    h_prev = h_sc[...]
    hs = []
    for g in range(groups):
        hs.append(b[g] + a[g] * h_prev)
        h_prev = hs[-1][SUBLANES - 1:SUBLANES, :]
    h_sc[...] = h_prev
    h = jnp.concatenate(hs, axis=0)
    out_ref[...] = (h * _gelu_tanh(y_ref[...].astype(F32))).astype(BF16)


def _rg_branch(z, cw, cb, wa, ba, wx, bx, lam):
    t = z.shape[0]
    tm = min(TM_RG, t)
    full = lambda shape: pl.BlockSpec(shape, lambda i: (0,) * len(shape))
    return pl.pallas_call(
        _rg_kernel,
        grid=(t // tm,),
        in_specs=[
            pl.BlockSpec((tm, RG_WIDTH), lambda i: (i, Z_RGX)),
            pl.BlockSpec((tm, RG_WIDTH), lambda i: (i, Z_RGY)),
            full((CONV_K, RG_WIDTH)),
            full((1, RG_WIDTH)),
            full((RG_BLOCKS, RG_BLOCK_DIM, RG_BLOCK_DIM)),
            full((1, RG_WIDTH)),
            full((RG_BLOCKS, RG_BLOCK_DIM, RG_BLOCK_DIM)),
            full((1, RG_WIDTH)),
            full((1, RG_WIDTH)),
            full(((CONV_K - 1) * tm, tm)),
        ],
        out_specs=pl.BlockSpec((tm, RG_WIDTH), lambda i: (i, 0)),
        out_shape=jax.ShapeDtypeStruct((t, RG_WIDTH), BF16),
        scratch_shapes=[pltpu.VMEM((SUBLANES, RG_WIDTH), F32), pltpu.VMEM((1, RG_WIDTH), F32)],
        compiler_params=_cparams(("arbitrary",)),
        name="rg_branch",
    )(z, z, cw, cb, wa, ba, wx, bx, lam, _delay_matrix(tm))


def _l2norm_heads(x, n_heads):
    parts = []
    for h in range(n_heads):
        xh = x[:, h * HEAD_DIM:(h + 1) * HEAD_DIM]
        parts.append(xh * lax.rsqrt(jnp.sum(xh * xh, axis=-1, keepdims=True) + EPS))
    return jnp.concatenate(parts, axis=1)


def _conv_silu(x_ref, prev_sc, cw_ref, delay_ref, lo):
    return _silu(_causal_conv(x_ref, prev_sc, cw_ref, delay_ref, lo))


def _unit_lower_inverses(mats):
    c = mats[0].shape[0]
    ri = lax.broadcasted_iota(I32, (c, c), 0)
    ci = lax.broadcasted_iota(I32, (c, c), 1)
    eye = jnp.where(ri == ci, 1.0, 0.0)
    ps = [eye - a for a in mats]
    xbs = [a.astype(BF16) for a in mats]
    xbs = [_dot(xb, xb).astype(BF16) for xb in xbs]
    power = 2
    while power < c:
        ps = [p + _dot(p.astype(BF16), xb) for p, xb in zip(ps, xbs)]
        power *= 2
        if power < c:
            xbs = [_dot(xb, xb).astype(BF16) for xb in xbs]
    return ps


def _gdn_kernel(q_ref, k_ref, v_ref, cw_ref, ba_ref, bat_ref, arow_ref, acol_ref, msum_ref, delay_ref,
                o_ref, s_sc, gt_sc, prev_sc, *, nch):
    hg_n = V_HEADS
    c = CHUNK
    tile = nch * c
    scale = HEAD_DIM ** -0.5

    @pl.when(pl.program_id(0) == 0)
    def _():
        s_sc[...] = jnp.zeros_like(s_sc)
        prev_sc[...] = jnp.zeros_like(prev_sc)

    q_all = _l2norm_heads(_conv_silu(q_ref, prev_sc, cw_ref, delay_ref, 0), QK_HEADS)
    k_all = _l2norm_heads(_conv_silu(k_ref, prev_sc, cw_ref, delay_ref, KEY_DIM), QK_HEADS)
    v_all = _conv_silu(v_ref, prev_sc, cw_ref, delay_ref, 2 * KEY_DIM)
    q_bf, k_bf = q_all.astype(BF16), k_all.astype(BF16)

    ba = ba_ref[...]
    arow = arow_ref[...]
    g_all = -jnp.exp(arow[0:1, :]) * _softplus(ba + arow[1:2, :])
    beta_all = _sigmoid(ba)
    msum = msum_ref[...]
    sums = sum(_dot(msum, term) for term in _bf16_terms(g_all))
    gcs = sums[:tile, :]
    gls = sums[tile:, :]
    egc = jnp.exp(gcs)
    ekd = jnp.exp(gls - gcs)
    acol = acol_ref[...]
    g_t = -jnp.exp(acol[:, 0:1]) * _softplus(bat_ref[...] + acol[:, 1:2])
    gt_sc[...] = sum(_dot_nt(term, msum[:tile, :]) for term in _bf16_terms(g_t))

    lane = lax.broadcasted_iota(I32, (tile, LANES), 1)
    ri = lax.broadcasted_iota(I32, (c, c), 0)
    ci = lax.broadcasted_iota(I32, (c, c), 1)
    causal = ri >= ci
    strict = ri > ci

    def column(arr, lane_idx):
        return jnp.sum(jnp.where(lane == lane_idx, arr, 0.0), axis=1, keepdims=True)

    cols = []
    for j in range(hg_n):
        head = j
        b_col = column(beta_all, head)
        egc_col = column(egc, V_HEADS + head)
        cols.append(dict(
            b=b_col, gcs=column(gcs, V_HEADS + head), egc=egc_col,
            ekd=column(ekd, V_HEADS + head), be=b_col * egc_col,
            g_row=gt_sc[pl.ds(V_HEADS + head, 1), :]))

    pairs = [(n, j) for n in range(nch) for j in range(hg_n)]
    rows_of = lambda n: slice(n * c, (n + 1) * c)
    head_cols = lambda j: slice(j * HEAD_DIM, (j + 1) * HEAD_DIM)
    kq = {}
    for n in range(nch):
        for qh in range(hg_n // 2):
            k = k_bf[rows_of(n), head_cols(qh)]
            q = q_bf[rows_of(n), head_cols(qh)]
            kq[(n, qh)] = dict(kk=_dot_nt(k, k), qk=_dot_nt(q, k), kf=k_all[rows_of(n), head_cols(qh)],
                               qf=q_all[rows_of(n), head_cols(qh)])
    decays, a_mats = {}, []
    for n, j in pairs:
        col = cols[j]
        diff = col["gcs"][rows_of(n), :] - col["g_row"][:, rows_of(n)]
        decay = jnp.where(causal, jnp.exp(jnp.where(causal, diff, 0.0)), 0.0)
        decays[(n, j)] = decay
        a_mats.append(jnp.where(strict, col["b"][rows_of(n), :] * kq[(n, j // 2)]["kk"] * decay, 0.0))
    t_mats = _unit_lower_inverses(a_mats)
    uws, attns, q_decs, k_decs = {}, {}, {}, {}
    for (n, j), t_mat in zip(pairs, t_mats):
        col, kqn = cols[j], kq[(n, j // 2)]
        v = v_all[rows_of(n), head_cols(j)]
        rhs = jnp.concatenate([v * col["b"][rows_of(n), :], kqn["kf"] * col["be"][rows_of(n), :]], axis=1)
        uws[(n, j)] = _dot(t_mat.astype(BF16), rhs.astype(BF16))
        attns[(n, j)] = (kqn["qk"] * decays[(n, j)]).astype(BF16)
        q_decs[(n, j)] = kqn["qf"] * col["egc"][rows_of(n), :]
        k_decs[(n, j)] = (kqn["kf"] * col["ekd"][rows_of(n), :]).astype(BF16)

    states = [s_sc[j] for j in range(hg_n)]
    for n in range(nch):
        heads = range(hg_n)
        ws_qs = [_dot(jnp.concatenate([uws[(n, j)][:, HEAD_DIM:], q_decs[(n, j)]], axis=0).astype(BF16),
                      states[j].astype(BF16)) for j in heads]
        v_news = [(uws[(n, j)][:, :HEAD_DIM] - ws_qs[j][:c, :]).astype(BF16) for j in heads]
        outs = [(ws_qs[j][c:, :] + _dot(attns[(n, j)], v_news[j])) * scale for j in heads]
        for j in heads:
            o_ref[rows_of(n), head_cols(j)] = outs[j].astype(BF16)
        egls = [jnp.exp(cols[j]["g_row"][:, (n + 1) * c - 1:(n + 1) * c]) for j in heads]
        states = [states[j] * egls[j] + _dot_tn(k_decs[(n, j)], v_news[j]) for j in heads]
    for j in range(hg_n):
        s_sc[j] = states[j]


def _gdn_branch(z, cw, ba, bat, arow, acol):
    t = z.shape[0]
    nch = GDN_NCH
    tile = nch * CHUNK
    idx = jnp.arange(tile)
    same = (idx[:, None] // CHUNK) == (idx[None, :] // CHUNK)
    mtri = jnp.logical_and(same, idx[:, None] >= idx[None, :])
    msum = jnp.concatenate([mtri, same], axis=0).astype(BF16)
    return pl.pallas_call(
        functools.partial(_gdn_kernel, nch=nch),
        grid=(t // tile,),
        in_specs=[
            pl.BlockSpec((tile, KEY_DIM), lambda n: (n, Z_Q)),
            pl.BlockSpec((tile, KEY_DIM), lambda n: (n, Z_K)),
            pl.BlockSpec((tile, VAL_DIM), lambda n: (n, Z_V)),
            pl.BlockSpec((CONV_K, 2 * KEY_DIM + VAL_DIM), lambda n: (0, 0)),
            pl.BlockSpec((tile, LANES), lambda n: (n, 0)),
            pl.BlockSpec((LANES, tile), lambda n: (0, n)),
            pl.BlockSpec((2, LANES), lambda n: (0, 0)),
            pl.BlockSpec((LANES, 2), lambda n: (0, 0)),
            pl.BlockSpec((2 * tile, tile), lambda n: (0, 0)),
            pl.BlockSpec(((CONV_K - 1) * tile, tile), lambda n: (0, 0)),
        ],
        out_specs=pl.BlockSpec((tile, VAL_DIM), lambda n: (n, 0)),
        out_shape=jax.ShapeDtypeStruct((t, VAL_DIM), BF16),
        scratch_shapes=[pltpu.VMEM((V_HEADS, HEAD_DIM, HEAD_DIM), F32), pltpu.VMEM((LANES, tile), F32),
                        pltpu.VMEM((SUBLANES, 2 * KEY_DIM + VAL_DIM), F32)],
        compiler_params=_cparams(("arbitrary",)),
        name="gdn_branch",
    )(z, z, z, cw, ba, bat, arow, acol, msum, _delay_matrix(tile))


def _merge_kernel(x_ref, hg_ref, o_ref, zg_ref, ga_ref, gb_ref, wrg_ref, wgdn_ref, wout_ref,
                  gnw_ref, fnw_ref, rw_ref, rb_ref, lstrict_ref,
                  h_ref, hn_ref, sel_ref, gate_ref, cnt_ref, carry_sc):
    tm = x_ref.shape[0]

    @pl.when(pl.program_id(0) == 0)
    def _():
        carry_sc[...] = jnp.zeros_like(carry_sc)

    ya = _dot(hg_ref[...], wrg_ref[...])
    o = o_ref[...].astype(F32)
    zg = zg_ref[...].astype(F32)
    gnw = gnw_ref[...]
    parts = []
    for h in range(V_HEADS):
        sl = slice(h * HEAD_DIM, (h + 1) * HEAD_DIM)
        parts.append((_rms(o[:, sl], gnw) * _silu(zg[:, sl])).astype(BF16))
    yb = _dot(jnp.concatenate(parts, axis=1), wgdn_ref[...])
    mix = _sigmoid(ga_ref[...].astype(F32)) * ya + _sigmoid(gb_ref[...].astype(F32)) * yb
    h = x_ref[...] + _dot(mix.astype(BF16), wout_ref[...])
    h_ref[...] = h

    hn = _rms(h, fnw_ref[...])
    hn_ref[...] = hn
    lane = lax.broadcasted_iota(I32, (tm, LANES), 1)
    lane_f = lane.astype(F32)
    neg_inf = float("-inf")
    hn_hi = hn.astype(BF16)
    hn_lo = (hn - hn_hi.astype(F32)).astype(BF16)
    rw = rw_ref[...]
    hi = _dot(hn_hi, rw)
    raw = hi[:, :LANES] + hi[:, LANES:] + _dot(hn_lo, rw[:, :LANES])
    logits = jnp.where(lane < N_EXPERTS, raw + rb_ref[...], neg_inf)
    vals, idxs = [], []
    multi_hot = jnp.zeros((tm, LANES), F32)
    for _ in range(TOP_K):
        m = jnp.max(logits, axis=-1, keepdims=True)
        first = jnp.min(jnp.where(logits == m, lane_f, float(LANES)), axis=-1, keepdims=True)
        hit = lane_f == first
        multi_hot = multi_hot + jnp.where(hit, 1.0, 0.0)
        logits = jnp.where(hit, neg_inf, logits)
        vals.append(m)
        idxs.append(first)
    exps = [jnp.exp(v - vals[0]) for v in vals]
    denom = exps[0] + exps[1] + exps[2] + exps[3]

    before = _dot(lstrict_ref[...], multi_hot.astype(BF16)) + carry_sc[...]
    carry_sc[...] = carry_sc[...] + jnp.sum(multi_hot, axis=0, keepdims=True)
    cnt_ref[...] = carry_sc[...].astype(I32)

    sel = jnp.zeros((tm, LANES), F32)
    gate = jnp.zeros((tm, LANES), F32)
    for kk in range(TOP_K):
        rank = jnp.sum(jnp.where(lane_f == idxs[kk], before, 0.0), axis=-1, keepdims=True)
        sel = jnp.where(lane == kk, idxs[kk], sel)
        sel = jnp.where(lane == TOP_K + kk, rank, sel)
        gate = jnp.where(lane == kk, exps[kk] / denom, gate)
    sel_ref[...] = sel.astype(I32)
    gate_ref[...] = gate


def _merge_router(x, hg, o, z, wrg, wgdn, wout, gnw, fnw, rw, rb):
    t = x.shape[0]
    tm = min(TM_MERGE, t)
    idx = jnp.arange(tm)
    lstrict = (idx[:, None] > idx[None, :]).astype(BF16)
    full = lambda shape: pl.BlockSpec(shape, lambda i: (0,) * len(shape))
    row = lambda width, col=0: pl.BlockSpec((tm, width), lambda i: (i, col))
    return pl.pallas_call(
        _merge_kernel,
        grid=(t // tm,),
        in_specs=[
            row(D_MODEL), row(RG_WIDTH), row(VAL_DIM), row(VAL_DIM, Z_ZG), row(D_MODEL, Z_GA),
            row(D_MODEL, Z_GB),
            full((RG_WIDTH, D_MODEL)), full((VAL_DIM, D_MODEL)), full((D_MODEL, D_MODEL)),
            full((1, HEAD_DIM)), full((1, D_MODEL)), full((D_MODEL, 2 * LANES)), full((1, LANES)),
            full((tm, tm)),
        ],
        out_specs=[row(D_MODEL), row(D_MODEL), row(LANES), row(LANES), full((1, LANES))],
        out_shape=[
            jax.ShapeDtypeStruct((t, D_MODEL), F32),
            jax.ShapeDtypeStruct((t, D_MODEL), F32),
            jax.ShapeDtypeStruct((t, LANES), I32),
            jax.ShapeDtypeStruct((t, LANES), F32),
            jax.ShapeDtypeStruct((1, LANES), I32),
        ],
        scratch_shapes=[pltpu.VMEM((1, LANES), F32)],
        compiler_params=_cparams(("arbitrary",)),
        name="merge_router",
    )(x, hg, o, z, z, z, wrg, wgdn, wout, gnw, fnw, rw, rb, lstrict)


def _dest_kernel(sel_ref, pstart_ref, dest_ref):
    tm = sel_ref.shape[0]
    sel = sel_ref[...].astype(F32)
    lane = lax.broadcasted_iota(I32, (tm, LANES), 1)
    lane_f = lane.astype(F32)
    pstart = pstart_ref[...].astype(F32)
    dest = jnp.zeros((tm, LANES), F32)
    for kk in range(TOP_K):
        e = jnp.sum(jnp.where(lane == kk, sel, 0.0), axis=-1, keepdims=True)
        rank = jnp.sum(jnp.where(lane == TOP_K + kk, sel, 0.0), axis=-1, keepdims=True)
        base = jnp.sum(jnp.where(lane_f == e, pstart, 0.0), axis=-1, keepdims=True)
        dest = jnp.where(lane == kk, base + rank, dest)
    dest_ref[...] = dest.astype(I32)


def _dest_rows(sel, pstart):
    t = sel.shape[0]
    tm = min(1024, t)
    return pl.pallas_call(
        _dest_kernel,
        grid=(t // tm,),
        in_specs=[pl.BlockSpec((tm, LANES), lambda i: (i, 0)), pl.BlockSpec((1, LANES), lambda i: (0, 0))],
        out_specs=pl.BlockSpec((tm, LANES), lambda i: (i, 0)),
        out_shape=jax.ShapeDtypeStruct((t, LANES), I32),
        compiler_params=_cparams(("arbitrary",)),
        name="dest_rows",
    )(sel, pstart)


def _row_copy(src_ref, src_row, dst_ref, dst_row, sem):
    return pltpu.make_async_copy(src_ref.at[pl.ds(src_row, 1), :], dst_ref.at[pl.ds(dst_row, 1), :], sem)


def _dispatch_kernel(dest_ref, zflag_ref, hn_ref, xr_ref, zero_sc, sem, zsem):
    tm = hn_ref.shape[0]
    n_tiles = zflag_ref.shape[0]

    def zero_copy(tile):
        start = pl.multiple_of(tile * MOE_TILE, MOE_TILE)
        return pltpu.make_async_copy(zero_sc, xr_ref.at[pl.ds(start, MOE_TILE), :], zsem)

    @pl.when(pl.program_id(0) == 0)
    def _():
        zero_sc[...] = jnp.zeros_like(zero_sc)

        def start(tile, carry):
            @pl.when(zflag_ref[tile] > 0)
            def _():
                zero_copy(tile).start()
            return carry

        def wait(tile, carry):
            @pl.when(zflag_ref[tile] > 0)
            def _():
                zero_copy(tile).wait()
            return carry

        lax.fori_loop(0, n_tiles, start, 0)
        lax.fori_loop(0, n_tiles, wait, 0)

    def issue(tok, carry):
        for kk in range(TOP_K):
            _row_copy(hn_ref, tok, xr_ref, dest_ref[tok * TOP_K + kk], sem).start()
        return carry

    lax.fori_loop(0, tm, issue, 0)

    def drain(tok, carry):
        for kk in range(TOP_K):
            _row_copy(hn_ref, tok, xr_ref, dest_ref[tok * TOP_K + kk], sem).wait()
        return carry

    lax.fori_loop(0, tm, drain, 0)


def _dispatch(dest_flat, zflag, hn):
    t = hn.shape[0]
    tm = min(TM_ROWS, t)
    n_tiles = zflag.shape[0]
    return pl.pallas_call(
        _dispatch_kernel,
        grid=(t // tm,),
        in_specs=[
            pl.BlockSpec((tm * TOP_K,), lambda i: (i,), memory_space=pltpu.SMEM),
            pl.BlockSpec((n_tiles,), lambda i: (0,), memory_space=pltpu.SMEM),
            pl.BlockSpec((tm, D_MODEL), lambda i: (i, 0)),
        ],
        out_specs=pl.BlockSpec(memory_space=pl.ANY),
        out_shape=jax.ShapeDtypeStruct((n_tiles * MOE_TILE, D_MODEL), F32),
        scratch_shapes=[pltpu.VMEM((MOE_TILE, D_MODEL), F32), pltpu.SemaphoreType.DMA,
                        pltpu.SemaphoreType.DMA],
        compiler_params=_cparams(("arbitrary",)),
        name="dispatch",
    )(dest_flat, zflag, hn)


def _expert_kernel(te_ref, tv_ref, tf_ref, tn_ref, ts_ref, x_ref, bgu_ref, bd_ref, wgu_hbm, wd_hbm, y_ref,
                   wgu_f32, wd_f32, wgu_sc, wd_sc, sems):
    i = pl.program_id(0)

    def weight_copies(expert, slot):
        return (pltpu.make_async_copy(wgu_hbm.at[expert], wgu_f32.at[slot], sems.at[0, slot]),
                pltpu.make_async_copy(wd_hbm.at[expert], wd_f32.at[slot], sems.at[1, slot]))

    @pl.when(i == 0)
    def _():
        for copy in weight_copies(te_ref[0], ts_ref[0]):
            copy.start()

    @pl.when(tf_ref[i] > 0)
    def _():
        slot = ts_ref[i]
        for copy in weight_copies(te_ref[i], slot):
            copy.wait()

        @pl.when(tn_ref[i] >= 0)
        def _():
            for copy in weight_copies(tn_ref[i], 1 - slot):
                copy.start()

        wgu_sc[...] = wgu_f32[slot].astype(BF16)
        wd_sc[...] = wd_f32[slot].astype(BF16)

    @pl.when(tv_ref[i] > 0)
    def _():
        gu = _dot(x_ref[...].astype(BF16), wgu_sc[...]) + bgu_ref[0]
        gate = jnp.minimum(gu[:, :D_EXPERT], SWIGLU_LIMIT)
        up = jnp.clip(gu[:, D_EXPERT:], -SWIGLU_LIMIT, SWIGLU_LIMIT)
        hid = (up + 1.0) * (gate * _sigmoid(SWIGLU_ALPHA * gate))
        y_ref[...] = _dot(hid.astype(BF16), wd_sc[...]) + bd_ref[0]

    @pl.when(tv_ref[i] == 0)
    def _():
        y_ref[...] = jnp.zeros_like(y_ref)


def _experts(tile_e, tile_valid, tile_first, tile_next, tile_slot, x_rows, wgu, bgu, wd, bd):
    n_rows = x_rows.shape[0]
    n_tiles = n_rows // MOE_TILE
    by_expert = lambda i, te, tv, tf, tn, ts: (te[i], 0, 0)
    grid_spec = pltpu.PrefetchScalarGridSpec(
        num_scalar_prefetch=5,
        grid=(n_tiles,),
        in_specs=[
            pl.BlockSpec((MOE_TILE, D_MODEL), lambda i, te, tv, tf, tn, ts: (i, 0)),
            pl.BlockSpec((1, 1, 2 * D_EXPERT), by_expert),
            pl.BlockSpec((1, 1, D_MODEL), by_expert),
            pl.BlockSpec(memory_space=pl.ANY),
            pl.BlockSpec(memory_space=pl.ANY),
        ],
        out_specs=pl.BlockSpec((MOE_TILE, D_MODEL), lambda i, te, tv, tf, tn, ts: (i, 0)),
        scratch_shapes=[
            pltpu.VMEM((2, D_MODEL, 2 * D_EXPERT), F32), pltpu.VMEM((2, D_EXPERT, D_MODEL), F32),
            pltpu.VMEM((D_MODEL, 2 * D_EXPERT), BF16), pltpu.VMEM((D_EXPERT, D_MODEL), BF16),
            pltpu.SemaphoreType.DMA((2, 2)),
        ],
    )
    return pl.pallas_call(
        _expert_kernel,
        grid_spec=grid_spec,
        out_shape=jax.ShapeDtypeStruct((n_rows, D_MODEL), F32),
        compiler_params=_cparams(("arbitrary",)),
        name="experts",
    )(tile_e, tile_valid, tile_first, tile_next, tile_slot, x_rows, bgu, bd, wgu, wd)


def _combine_kernel(dest_ref, h_ref, gate_ref, nw_ref, yr_ref, out_ref, buf_sc, sem):
    tm = h_ref.shape[0]

    def row_copy(tok, kk):
        return _row_copy(yr_ref, dest_ref[tok * TOP_K + kk], buf_sc.at[kk], tok, sem)

    def issue(tok, carry):
        for kk in range(TOP_K):
            row_copy(tok, kk).start()
        return carry

    lax.fori_loop(0, tm, issue, 0)

    def drain(tok, carry):
        for kk in range(TOP_K):
            row_copy(tok, kk).wait()
        return carry

    lax.fori_loop(0, tm, drain, 0)

    gate = gate_ref[...]
    acc = h_ref[...]
    for kk in range(TOP_K):
        acc = acc + gate[:, kk:kk + 1] * buf_sc[kk]
    out_ref[...] = _rms(acc, nw_ref[...])


def _combine(dest_flat, h, gate, nw, y_rows):
    t = h.shape[0]
    tm = min(TM_ROWS, t)
    return pl.pallas_call(
        _combine_kernel,
        grid=(t // tm,),
        in_specs=[
            pl.BlockSpec((tm * TOP_K,), lambda i: (i,), memory_space=pltpu.SMEM),
            pl.BlockSpec((tm, D_MODEL), lambda i: (i, 0)),
            pl.BlockSpec((tm, LANES), lambda i: (i, 0)),
            pl.BlockSpec((1, D_MODEL), lambda i: (0, 0)),
            pl.BlockSpec(memory_space=pl.ANY),
        ],
        out_specs=pl.BlockSpec((tm, D_MODEL), lambda i: (i, 0)),
        out_shape=jax.ShapeDtypeStruct((t, D_MODEL), F32),
        scratch_shapes=[pltpu.VMEM((TOP_K, tm, D_MODEL), F32), pltpu.SemaphoreType.DMA],
        compiler_params=_cparams(("arbitrary",)),
        name="combine",
    )(dest_flat, h, gate, nw, y_rows)


def _pad_lanes(a, offset, axis):
    n = a.shape[axis]
    pads = [(0, 0)] * a.ndim
    pads[axis] = (offset, LANES - offset - n)
    return jnp.pad(a, pads)


def _layer(x, norm_mix_w, w_in, rg_conv_w, rg_conv_b, rg_gate_a_w, rg_gate_a_b, rg_gate_x_w,
           rg_gate_x_b, rg_lambda, gdn_conv_w, gdn_a_log, gdn_dt_bias, gdn_norm_w, w_branch_rg,
           w_branch_gdn, w_out, norm_ffn_w, router_w, router_b, moe_w_gate_up, moe_b_gate_up,
           moe_w_down, moe_b_down, out_norm_w):
    t = x.shape[0]
    row = lambda a: a.reshape(1, -1).astype(F32)

    small_lo = 2 * RG_WIDTH + 2 * KEY_DIM + 2 * VAL_DIM
    small_hi = small_lo + 2 * V_HEADS
    w_z = jnp.concatenate([w_in[:, :small_lo].astype(BF16), w_in[:, small_hi:].astype(BF16)], axis=1)
    w_ba = _pad_lanes(w_in[:, small_lo:small_hi], 0, 1)
    w_ba_hi = w_ba.astype(BF16)
    w_ba_lo = (w_ba - w_ba_hi.astype(F32)).astype(BF16)
    z, ba, bat = _inproj(x, row(norm_mix_w), w_z, jnp.concatenate([w_ba_hi, w_ba_lo], axis=1))

    hg = _rg_branch(z, rg_conv_w, row(rg_conv_b), rg_gate_a_w.astype(BF16), row(rg_gate_a_b),
                    rg_gate_x_w.astype(BF16), row(rg_gate_x_b), row(rg_lambda))

    arow = jnp.stack([_pad_lanes(gdn_a_log, V_HEADS, 0), _pad_lanes(gdn_dt_bias, V_HEADS, 0)])
    o = _gdn_branch(z, gdn_conv_w, ba, bat, arow, arow.T)

    rw = _pad_lanes(router_w, 0, 1)
    rw_hi = rw.astype(BF16)
    rw = jnp.concatenate([rw_hi, (rw - rw_hi.astype(F32)).astype(BF16)], axis=1)
    rb = _pad_lanes(router_b, 0, 0).reshape(1, LANES)
    h, hn, sel, gate, counts = _merge_router(
        x, hg, o, z, w_branch_rg.astype(BF16), w_branch_gdn.astype(BF16), w_out.astype(BF16),
        row(gdn_norm_w), row(norm_ffn_w), rw, rb)

    counts = counts[0, :N_EXPERTS]
    padded = ((counts + MOE_TILE - 1) // MOE_TILE) * MOE_TILE
    pends = jnp.cumsum(padded)
    pstarts = pends - padded
    n_tiles = (t * TOP_K) // MOE_TILE + N_EXPERTS
    tile_start = jnp.arange(n_tiles, dtype=I32) * MOE_TILE
    tile_valid = (tile_start < pends[-1]).astype(I32)
    tile_e = jnp.minimum(jnp.sum(pends[None, :] <= tile_start[:, None], axis=1), N_EXPERTS - 1).astype(I32)
    prev_e = jnp.concatenate([jnp.full((1,), -1, I32), tile_e[:-1]])
    tile_first = jnp.logical_and(tile_e != prev_e, tile_valid > 0).astype(I32)
    tile_slot = ((jnp.cumsum(tile_first) - 1) % 2).astype(I32)
    first_pos = jnp.where(tile_first > 0, jnp.arange(n_tiles, dtype=I32), n_tiles)
    next_first = jnp.concatenate([lax.cummin(first_pos[::-1])[::-1][1:], jnp.full((1,), n_tiles, I32)])
    tile_next = jnp.where(next_first < n_tiles, tile_e[jnp.minimum(next_first, n_tiles - 1)], -1).astype(I32)
    group_end = jnp.any(jnp.logical_and((tile_start + MOE_TILE)[:, None] == pends[None, :],
                                        padded[None, :] > 0), axis=1)
    zflag = jnp.logical_or(group_end, tile_valid == 0).astype(I32)

    dest = _dest_rows(sel, _pad_lanes(pstarts.astype(I32), 0, 0).reshape(1, LANES))
    dest_flat = dest[:, :TOP_K].reshape(t * TOP_K)

    x_rows = _dispatch(dest_flat, zflag, hn)
    y_rows = _experts(tile_e, tile_valid, tile_first, tile_next, tile_slot, x_rows, moe_w_gate_up,
                      moe_b_gate_up.reshape(N_EXPERTS, 1, 2 * D_EXPERT), moe_w_down,
                      moe_b_down.reshape(N_EXPERTS, 1, D_MODEL))

    return _combine(dest_flat, h, gate, row(out_norm_w), y_rows)


def kernel(x, norm_mix_w, w_in, rg_conv_w, rg_conv_b, rg_gate_a_w, rg_gate_a_b, rg_gate_x_w,
           rg_gate_x_b, rg_lambda, gdn_conv_w, gdn_A_log, gdn_dt_bias, gdn_norm_w, w_branch_rg,
           w_branch_gdn, w_out, norm_ffn_w, router_w, router_b, moe_w_gate_up, moe_b_gate_up,
           moe_w_down, moe_b_down, norm_final_w):
    assert x.shape[0] == 1 and norm_mix_w.shape[0] == 1
    out = _layer(x[0], norm_mix_w[0], w_in[0], rg_conv_w[0], rg_conv_b[0], rg_gate_a_w[0],
                 rg_gate_a_b[0], rg_gate_x_w[0], rg_gate_x_b[0], rg_lambda[0], gdn_conv_w[0],
                 gdn_A_log[0], gdn_dt_bias[0], gdn_norm_w[0], w_branch_rg[0], w_branch_gdn[0],
                 w_out[0], norm_ffn_w[0], router_w[0], router_b[0], moe_w_gate_up[0],
                 moe_b_gate_up[0], moe_w_down[0], moe_b_down[0], norm_final_w)
    return out[None]
```

```python
import functools

import jax
import jax.numpy as jnp
from jax import lax
from jax.experimental import pallas as pl
from jax.experimental.pallas import tpu as pltpu

F32 = jnp.float32
BF16 = jnp.bfloat16
I32 = jnp.int32
HI = lax.Precision.HIGHEST

D_MODEL = 1024
RG_WIDTH = 1024
RG_BLOCKS = 4
RG_BLOCK_DIM = RG_WIDTH // RG_BLOCKS
RG_C = 8.0
CONV_K = 4
QK_HEADS = 8
V_HEADS = 16
HEAD_DIM = 128
KEY_DIM = QK_HEADS * HEAD_DIM
VAL_DIM = V_HEADS * HEAD_DIM
CHUNK = 64
N_EXPERTS = 32
TOP_K = 4
D_EXPERT = 1024
SWIGLU_ALPHA = 1.702
SWIGLU_LIMIT = 7.0
EPS = 1e-6

LANES = 128
SUBLANES = 8
VMEM_LIMIT = 56 * 1024 * 1024

Z_RGX, Z_RGY, Z_Q, Z_K = 0, 1, 2, 3
Z_V, Z_ZG = 2, 3
Z_GA, Z_GB = 8, 9
Z_WIDTH = 10 * 1024

TM_INPROJ = 1024
TN_INPROJ = 2048
TM_RG = 256
GDN_NCH = 2
TM_MERGE = 512
TM_ROWS = 512
MOE_TILE = 512


def _cparams(sem):
    return pltpu.CompilerParams(dimension_semantics=sem, vmem_limit_bytes=VMEM_LIMIT)


def _softplus(x):
    return jnp.maximum(x, 0.0) + jnp.log1p(jnp.exp(-jnp.abs(x)))


def _sigmoid(x):
    return 1.0 / (1.0 + jnp.exp(-x))


def _silu(x):
    return x * _sigmoid(x)


def _gelu_tanh(x):
    c = 0.7978845608028654
    return 0.5 * x * (1.0 + jnp.tanh(c * (x + 0.044715 * (x * x * x))))


def _rms(x, w):
    return x * lax.rsqrt(jnp.mean(x * x, axis=-1, keepdims=True) + EPS) * w


def _dot(a, b):
    return jnp.dot(a, b, preferred_element_type=F32)


ROW_TILES = D_MODEL // LANES


def _rows_to_tiles(x):
    parts = jnp.stack([x[:, s * LANES:(s + 1) * LANES] for s in range(ROW_TILES)], axis=0)
    return pltpu.einshape("snl->nsl", parts)


def _tiles_to_rows(x):
    parts = pltpu.einshape("nsl->snl", x)
    return jnp.concatenate([parts[s] for s in range(ROW_TILES)], axis=1)


def _dot_hi(a, b):
    return jnp.dot(a, b, precision=HI, preferred_element_type=F32)


def _bf16_terms(x):
    t0 = x.astype(BF16)
    r = x - t0.astype(F32)
    t1 = r.astype(BF16)
    t2 = (r - t1.astype(F32)).astype(BF16)
    return t0, t1, t2


def _dot_nt(a, b, precision=None):
    return lax.dot_general(a, b, (((1,), (1,)), ((), ())), precision=precision,
                           preferred_element_type=F32)


def _dot_tn(a, b):
    return lax.dot_general(a, b, (((0,), (0,)), ((), ())), preferred_element_type=F32)


def _inproj_kernel(x_ref, nw_ref, w_ref, wba_ref, z_ref, ba_ref, bat_ref, xn_sc):
    @pl.when(pl.program_id(1) == 0)
    def _():
        xn = _rms(x_ref[...], nw_ref[...])
        x_hi = xn.astype(BF16)
        xn_sc[...] = x_hi
        x_lo = (xn - x_hi.astype(F32)).astype(BF16)
        wba = wba_ref[...]
        hi = _dot(x_hi, wba)
        ba = hi[:, :LANES] + hi[:, LANES:] + _dot(x_lo, wba[:, :LANES])
        ba_ref[...] = ba
        bat_ref[...] = ba.T

    z_ref[...] = _dot(xn_sc[...], w_ref[...]).astype(BF16)


def _inproj(x, nw, w_z, w_ba):
    t = x.shape[0]
    tm, tn = min(TM_INPROJ, t), TN_INPROJ
    return pl.pallas_call(
        _inproj_kernel,
        grid=(t // tm, Z_WIDTH // tn),
        in_specs=[
            pl.BlockSpec((tm, D_MODEL), lambda i, j: (i, 0)),
            pl.BlockSpec((1, D_MODEL), lambda i, j: (0, 0)),
            pl.BlockSpec((D_MODEL, tn), lambda i, j: (0, j)),
            pl.BlockSpec((D_MODEL, 2 * LANES), lambda i, j: (0, 0)),
        ],
        out_specs=[
            pl.BlockSpec((tm, tn), lambda i, j: (i, j)),
            pl.BlockSpec((tm, LANES), lambda i, j: (i, 0)),
            pl.BlockSpec((LANES, tm), lambda i, j: (0, i)),
        ],
        out_shape=[
            jax.ShapeDtypeStruct((t, Z_WIDTH), BF16),
            jax.ShapeDtypeStruct((t, LANES), F32),
            jax.ShapeDtypeStruct((LANES, t), F32),
        ],
        scratch_shapes=[pltpu.VMEM((tm, D_MODEL), BF16)],
        compiler_params=_cparams(("arbitrary", "arbitrary")),
        name="inproj",
    )(x, nw, w_z, w_ba)


def _delay_matrix(tm):
    t = jnp.arange(tm)
    return jnp.concatenate([(t[None, :] == t[:, None] - s) for s in range(1, CONV_K)], axis=0).astype(BF16)


def _causal_conv(x_ref, hist_sc, cw_ref, delay_ref, lo):
    tm, width = x_ref.shape
    cols = slice(lo, lo + width)
    xb = x_ref[...]
    x = xb.astype(F32)
    delayed = _dot(delay_ref[...], xb)
    acc = cw_ref[CONV_K - 1:CONV_K, cols] * x
    prev = hist_sc[:, cols]
    row = lax.broadcasted_iota(I32, (SUBLANES, 1), 0)
    head = jnp.zeros((SUBLANES, width), F32)
    for s in range(1, CONV_K):
        tap = cw_ref[CONV_K - 1 - s:CONV_K - s, cols]
        acc = acc + tap * delayed[(s - 1) * tm:s * tm, :]
        head = head + tap * jnp.where(row < s, pltpu.roll(prev, s, axis=0), 0.0)
    hist_sc[:, cols] = x[tm - SUBLANES:, :]
    return jnp.concatenate([acc[:SUBLANES, :] + head, acc[SUBLANES:, :]], axis=0)


def _rg_kernel(x_ref, y_ref, cw_ref, cb_ref, wa_ref, ba_ref, wx_ref, bx_ref, lam_ref, delay_ref,
               out_ref, prev_sc, h_sc):
    i = pl.program_id(0)
    tm = x_ref.shape[0]

    @pl.when(i == 0)
    def _():
        prev_sc[...] = jnp.zeros_like(prev_sc)
        h_sc[...] = jnp.zeros_like(h_sc)

    xa = _causal_conv(x_ref, prev_sc, cw_ref, delay_ref, 0) + cb_ref[...]

    xab = xa.astype(BF16)
    r_parts, i_parts = [], []
    for blk in range(RG_BLOCKS):
        xs = xab[:, blk * RG_BLOCK_DIM:(blk + 1) * RG_BLOCK_DIM]
        r_parts.append(_dot(xs, wa_ref[blk]))
        i_parts.append(_dot(xs, wx_ref[blk]))
    r = _sigmoid(jnp.concatenate(r_parts, axis=1) + ba_ref[...])
    ig = _sigmoid(jnp.concatenate(i_parts, axis=1) + bx_ref[...])

    log_a = (-RG_C) * r * _softplus(-lam_ref[...])
    a = jnp.exp(log_a)
    m2 = jnp.maximum(1.0 - a * a, 0.0)
    mult = jnp.where(m2 > 0.0, m2 * lax.rsqrt(m2), 0.0)
    rows = lax.broadcasted_iota(I32, (tm, 1), 0)
    mult = jnp.where(jnp.logical_and(rows == 0, i == 0), 1.0, mult)
    b = mult * (ig * xa)

    groups = tm // SUBLANES
    a = a.reshape(groups, SUBLANES, RG_WIDTH)
    b = b.reshape(groups, SUBLANES, RG_WIDTH)
    sub = lax.broadcasted_iota(I32, (1, SUBLANES, 1), 1)
    s = 1
    while s < SUBLANES:
        keep = sub >= s
        a_sh = pltpu.roll(a, s, axis=1)
        b_sh = pltpu.roll(b, s, axis=1)
        b = b + jnp.where(keep, a * b_sh, 0.0)
        a = jnp.where(keep, a * a_sh, a)
        s *= 2
    h_prev = h_sc[...]
    hs = []
    for g in range(groups):
        hs.append(b[g] + a[g] * h_prev)
        h_prev = hs[-1][SUBLANES - 1:SUBLANES, :]
    h_sc[...] = h_prev
    h = jnp.concatenate(hs, axis=0)
    out_ref[...] = (h * _gelu_tanh(y_ref[...].astype(F32))).astype(BF16)


def _rg_branch(z, cw, cb, wa, ba, wx, bx, lam):
    t = z.shape[0]
    tm = min(TM_RG, t)
    full = lambda shape: pl.BlockSpec(shape, lambda i: (0,) * len(shape))
    return pl.pallas_call(
        _rg_kernel,
        grid=(t // tm,),
        in_specs=[
            pl.BlockSpec((tm, RG_WIDTH), lambda i: (i, Z_RGX)),
            pl.BlockSpec((tm, RG_WIDTH), lambda i: (i, Z_RGY)),
            full((CONV_K, RG_WIDTH)),
            full((1, RG_WIDTH)),
            full((RG_BLOCKS, RG_BLOCK_DIM, RG_BLOCK_DIM)),
            full((1, RG_WIDTH)),
            full((RG_BLOCKS, RG_BLOCK_DIM, RG_BLOCK_DIM)),
            full((1, RG_WIDTH)),
            full((1, RG_WIDTH)),
            full(((CONV_K - 1) * tm, tm)),
        ],
        out_specs=pl.BlockSpec((tm, RG_WIDTH), lambda i: (i, 0)),
        out_shape=jax.ShapeDtypeStruct((t, RG_WIDTH), BF16),
        scratch_shapes=[pltpu.VMEM((SUBLANES, RG_WIDTH), F32), pltpu.VMEM((1, RG_WIDTH), F32)],
        compiler_params=_cparams(("arbitrary",)),
        name="rg_branch",
    )(z, z, cw, cb, wa, ba, wx, bx, lam, _delay_matrix(tm))


def _l2norm_heads(x, n_heads):
    parts = []
    for h in range(n_heads):
        xh = x[:, h * HEAD_DIM:(h + 1) * HEAD_DIM]
        parts.append(xh * lax.rsqrt(jnp.sum(xh * xh, axis=-1, keepdims=True) + EPS))
    return jnp.concatenate(parts, axis=1)


def _conv_silu(x_ref, prev_sc, cw_ref, delay_ref, lo):
    return _silu(_causal_conv(x_ref, prev_sc, cw_ref, delay_ref, lo))


def _unit_lower_inverses(mats):
    c = mats[0].shape[0]
    ri = lax.broadcasted_iota(I32, (c, c), 0)
    ci = lax.broadcasted_iota(I32, (c, c), 1)
    eye = jnp.where(ri == ci, 1.0, 0.0)
    ps = [eye - a for a in mats]
    xbs = [a.astype(BF16) for a in mats]
    xbs = [_dot(xb, xb).astype(BF16) for xb in xbs]
    power = 2
    while power < c:
        ps = [p + _dot(p.astype(BF16), xb) for p, xb in zip(ps, xbs)]
        power *= 2
        if power < c:
            xbs = [_dot(xb, xb).astype(BF16) for xb in xbs]
    return ps


def _gdn_kernel(q_ref, k_ref, v_ref, cw_ref, ba_ref, bat_ref, arow_ref, acol_ref, msum_ref, delay_ref,
                o_ref, s_sc, gt_sc, prev_sc, *, nch):
    hg_n = V_HEADS
    c = CHUNK
    tile = nch * c
    scale = HEAD_DIM ** -0.5

    @pl.when(pl.program_id(0) == 0)
    def _():
        s_sc[...] = jnp.zeros_like(s_sc)
        prev_sc[...] = jnp.zeros_like(prev_sc)

    q_all = _l2norm_heads(_conv_silu(q_ref, prev_sc, cw_ref, delay_ref, 0), QK_HEADS)
    k_all = _l2norm_heads(_conv_silu(k_ref, prev_sc, cw_ref, delay_ref, KEY_DIM), QK_HEADS)
    v_all = _conv_silu(v_ref, prev_sc, cw_ref, delay_ref, 2 * KEY_DIM)
    q_bf, k_bf = q_all.astype(BF16), k_all.astype(BF16)

    ba = ba_ref[...]
    arow = arow_ref[...]
    g_all = -jnp.exp(arow[0:1, :]) * _softplus(ba + arow[1:2, :])
    beta_all = _sigmoid(ba)
    msum = msum_ref[...]
    sums = sum(_dot(msum, term) for term in _bf16_terms(g_all))
    gcs = sums[:tile, :]
    gls = sums[tile:, :]
    egc = jnp.exp(gcs)
    ekd = jnp.exp(gls - gcs)
    acol = acol_ref[...]
    g_t = -jnp.exp(acol[:, 0:1]) * _softplus(bat_ref[...] + acol[:, 1:2])
    gt_sc[...] = sum(_dot_nt(term, msum[:tile, :]) for term in _bf16_terms(g_t))

    lane = lax.broadcasted_iota(I32, (tile, LANES), 1)
    ri = lax.broadcasted_iota(I32, (c, c), 0)
    ci = lax.broadcasted_iota(I32, (c, c), 1)
    causal = ri >= ci
    strict = ri > ci

    def column(arr, lane_idx):
        return jnp.sum(jnp.where(lane == lane_idx, arr, 0.0), axis=1, keepdims=True)

    cols = []
    for j in range(hg_n):
        head = j
        b_col = column(beta_all, head)
        egc_col = column(egc, V_HEADS + head)
        cols.append(dict(
            b=b_col, gcs=column(gcs, V_HEADS + head), egc=egc_col,
            ekd=column(ekd, V_HEADS + head), be=b_col * egc_col,
            g_row=gt_sc[pl.ds(V_HEADS + head, 1), :]))

    pairs = [(n, j) for n in range(nch) for j in range(hg_n)]
    rows_of = lambda n: slice(n * c, (n + 1) * c)
    head_cols = lambda j: slice(j * HEAD_DIM, (j + 1) * HEAD_DIM)
    kq = {}
    for n in range(nch):
        for qh in range(hg_n // 2):
            k = k_bf[rows_of(n), head_cols(qh)]
            q = q_bf[rows_of(n), head_cols(qh)]
            kq[(n, qh)] = dict(kk=_dot_nt(k, k), qk=_dot_nt(q, k), kf=k_all[rows_of(n), head_cols(qh)],
                               qf=q_all[rows_of(n), head_cols(qh)])
    decays, a_mats = {}, []
    for n, j in pairs:
        col = cols[j]
        diff = col["gcs"][rows_of(n), :] - col["g_row"][:, rows_of(n)]
        decay = jnp.where(causal, jnp.exp(jnp.where(causal, diff, 0.0)), 0.0)
        decays[(n, j)] = decay
        a_mats.append(jnp.where(strict, col["b"][rows_of(n), :] * kq[(n, j // 2)]["kk"] * decay, 0.0))
    t_mats = _unit_lower_inverses(a_mats)
    uws, attns, q_decs, k_decs = {}, {}, {}, {}
    for (n, j), t_mat in zip(pairs, t_mats):
        col, kqn = cols[j], kq[(n, j // 2)]
        v = v_all[rows_of(n), head_cols(j)]
        rhs = jnp.concatenate([v * col["b"][rows_of(n), :], kqn["kf"] * col["be"][rows_of(n), :]], axis=1)
        uws[(n, j)] = _dot(t_mat.astype(BF16), rhs.astype(BF16))
        attns[(n, j)] = (kqn["qk"] * decays[(n, j)]).astype(BF16)
        q_decs[(n, j)] = kqn["qf"] * col["egc"][rows_of(n), :]
        k_decs[(n, j)] = (kqn["kf"] * col["ekd"][rows_of(n), :]).astype(BF16)

    states = [s_sc[j] for j in range(hg_n)]
    for n in range(nch):
        heads = range(hg_n)
        ws_qs = [_dot(jnp.concatenate([uws[(n, j)][:, HEAD_DIM:], q_decs[(n, j)]], axis=0).astype(BF16),
                      states[j].astype(BF16)) for j in heads]
        v_news = [(uws[(n, j)][:, :HEAD_DIM] - ws_qs[j][:c, :]).astype(BF16) for j in heads]
        outs = [(ws_qs[j][c:, :] + _dot(attns[(n, j)], v_news[j])) * scale for j in heads]
        for j in heads:
            o_ref[rows_of(n), head_cols(j)] = outs[j].astype(BF16)
        egls = [jnp.exp(cols[j]["g_row"][:, (n + 1) * c - 1:(n + 1) * c]) for j in heads]
        states = [states[j] * egls[j] + _dot_tn(k_decs[(n, j)], v_news[j]) for j in heads]
    for j in range(hg_n):
        s_sc[j] = states[j]


def _gdn_branch(z, cw, ba, bat, arow, acol):
    t = z.shape[0]
    nch = GDN_NCH
    tile = nch * CHUNK
    idx = jnp.arange(tile)
    same = (idx[:, None] // CHUNK) == (idx[None, :] // CHUNK)
    mtri = jnp.logical_and(same, idx[:, None] >= idx[None, :])
    msum = jnp.concatenate([mtri, same], axis=0).astype(BF16)
    return pl.pallas_call(
        functools.partial(_gdn_kernel, nch=nch),
        grid=(t // tile,),
        in_specs=[
            pl.BlockSpec((tile, KEY_DIM), lambda n: (n, Z_Q)),
            pl.BlockSpec((tile, KEY_DIM), lambda n: (n, Z_K)),
            pl.BlockSpec((tile, VAL_DIM), lambda n: (n, Z_V)),
            pl.BlockSpec((CONV_K, 2 * KEY_DIM + VAL_DIM), lambda n: (0, 0)),
            pl.BlockSpec((tile, LANES), lambda n: (n, 0)),
            pl.BlockSpec((LANES, tile), lambda n: (0, n)),
            pl.BlockSpec((2, LANES), lambda n: (0, 0)),
            pl.BlockSpec((LANES, 2), lambda n: (0, 0)),
            pl.BlockSpec((2 * tile, tile), lambda n: (0, 0)),
            pl.BlockSpec(((CONV_K - 1) * tile, tile), lambda n: (0, 0)),
        ],
        out_specs=pl.BlockSpec((tile, VAL_DIM), lambda n: (n, 0)),
        out_shape=jax.ShapeDtypeStruct((t, VAL_DIM), BF16),
        scratch_shapes=[pltpu.VMEM((V_HEADS, HEAD_DIM, HEAD_DIM), F32), pltpu.VMEM((LANES, tile), F32),
                        pltpu.VMEM((SUBLANES, 2 * KEY_DIM + VAL_DIM), F32)],
        compiler_params=_cparams(("arbitrary",)),
        name="gdn_branch",
    )(z, z, z, cw, ba, bat, arow, acol, msum, _delay_matrix(tile))


def _merge_kernel(x_ref, hg_ref, o_ref, zg_ref, ga_ref, gb_ref, wrg_ref, wgdn_ref, wout_ref,
                  gnw_ref, fnw_ref, rw_ref, rb_ref, lstrict_ref,
                  h_ref, hn_ref, sel_ref, gate_ref, cnt_ref, carry_sc):
    tm = x_ref.shape[0]

    @pl.when(pl.program_id(0) == 0)
    def _():
        carry_sc[...] = jnp.zeros_like(carry_sc)

    ya = _dot(hg_ref[...], wrg_ref[...])
    o = o_ref[...].astype(F32)
    zg = zg_ref[...].astype(F32)
    gnw = gnw_ref[...]
    parts = []
    for h in range(V_HEADS):
        sl = slice(h * HEAD_DIM, (h + 1) * HEAD_DIM)
        parts.append((_rms(o[:, sl], gnw) * _silu(zg[:, sl])).astype(BF16))
    yb = _dot(jnp.concatenate(parts, axis=1), wgdn_ref[...])
    mix = _sigmoid(ga_ref[...].astype(F32)) * ya + _sigmoid(gb_ref[...].astype(F32)) * yb
    h = x_ref[...] + _dot(mix.astype(BF16), wout_ref[...])
    h_ref[...] = h

    hn = _rms(h, fnw_ref[...])
    hn_ref[...] = _rows_to_tiles(hn)
    lane = lax.broadcasted_iota(I32, (tm, LANES), 1)
    lane_f = lane.astype(F32)
    neg_inf = float("-inf")
    hn_hi = hn.astype(BF16)
    hn_lo = (hn - hn_hi.astype(F32)).astype(BF16)
    rw = rw_ref[...]
    hi = _dot(hn_hi, rw)
    raw = hi[:, :LANES] + hi[:, LANES:] + _dot(hn_lo, rw[:, :LANES])
    logits = jnp.where(lane < N_EXPERTS, raw + rb_ref[...], neg_inf)
    vals, idxs = [], []
    multi_hot = jnp.zeros((tm, LANES), F32)
    for _ in range(TOP_K):
        m = jnp.max(logits, axis=-1, keepdims=True)
        first = jnp.min(jnp.where(logits == m, lane_f, float(LANES)), axis=-1, keepdims=True)
        hit = lane_f == first
        multi_hot = multi_hot + jnp.where(hit, 1.0, 0.0)
        logits = jnp.where(hit, neg_inf, logits)
        vals.append(m)
        idxs.append(first)
    exps = [jnp.exp(v - vals[0]) for v in vals]
    denom = exps[0] + exps[1] + exps[2] + exps[3]

    before = _dot(lstrict_ref[...], multi_hot.astype(BF16)) + carry_sc[...]
    carry_sc[...] = carry_sc[...] + jnp.sum(multi_hot, axis=0, keepdims=True)
    cnt_ref[...] = carry_sc[...].astype(I32)

    sel = jnp.zeros((tm, LANES), F32)
    gate = jnp.zeros((tm, LANES), F32)
    for kk in range(TOP_K):
        rank = jnp.sum(jnp.where(lane_f == idxs[kk], before, 0.0), axis=-1, keepdims=True)
        sel = jnp.where(lane == kk, idxs[kk], sel)
        sel = jnp.where(lane == TOP_K + kk, rank, sel)
        gate = jnp.where(lane == kk, exps[kk] / denom, gate)
    sel_ref[...] = sel.astype(I32)
    gate_ref[...] = gate


def _merge_router(x, hg, o, z, wrg, wgdn, wout, gnw, fnw, rw, rb):
    t = x.shape[0]
    tm = min(TM_MERGE, t)
    idx = jnp.arange(tm)
    lstrict = (idx[:, None] > idx[None, :]).astype(BF16)
    full = lambda shape: pl.BlockSpec(shape, lambda i: (0,) * len(shape))
    row = lambda width, col=0: pl.BlockSpec((tm, width), lambda i: (i, col))
    return pl.pallas_call(
        _merge_kernel,
        grid=(t // tm,),
        in_specs=[
            row(D_MODEL), row(RG_WIDTH), row(VAL_DIM), row(VAL_DIM, Z_ZG), row(D_MODEL, Z_GA),
            row(D_MODEL, Z_GB),
            full((RG_WIDTH, D_MODEL)), full((VAL_DIM, D_MODEL)), full((D_MODEL, D_MODEL)),
            full((1, HEAD_DIM)), full((1, D_MODEL)), full((D_MODEL, 2 * LANES)), full((1, LANES)),
            full((tm, tm)),
        ],
        out_specs=[row(D_MODEL), pl.BlockSpec((tm, ROW_TILES, LANES), lambda i: (i, 0, 0)), row(LANES),
                   row(LANES), full((1, LANES))],
        out_shape=[
            jax.ShapeDtypeStruct((t, D_MODEL), F32),
            jax.ShapeDtypeStruct((t, ROW_TILES, LANES), F32),
            jax.ShapeDtypeStruct((t, LANES), I32),
            jax.ShapeDtypeStruct((t, LANES), F32),
            jax.ShapeDtypeStruct((1, LANES), I32),
        ],
        scratch_shapes=[pltpu.VMEM((1, LANES), F32)],
        compiler_params=_cparams(("arbitrary",)),
        name="merge_router",
    )(x, hg, o, z, z, z, wrg, wgdn, wout, gnw, fnw, rw, rb, lstrict)


def _dest_kernel(sel_ref, pstart_ref, dest_ref):
    tm = sel_ref.shape[0]
    sel = sel_ref[...].astype(F32)
    lane = lax.broadcasted_iota(I32, (tm, LANES), 1)
    lane_f = lane.astype(F32)
    pstart = pstart_ref[...].astype(F32)
    dest = jnp.zeros((tm, LANES), F32)
    for kk in range(TOP_K):
        e = jnp.sum(jnp.where(lane == kk, sel, 0.0), axis=-1, keepdims=True)
        rank = jnp.sum(jnp.where(lane == TOP_K + kk, sel, 0.0), axis=-1, keepdims=True)
        base = jnp.sum(jnp.where(lane_f == e, pstart, 0.0), axis=-1, keepdims=True)
        dest = jnp.where(lane == kk, base + rank, dest)
    dest_ref[...] = dest.astype(I32)


def _dest_rows(sel, pstart):
    t = sel.shape[0]
    tm = min(1024, t)
    return pl.pallas_call(
        _dest_kernel,
        grid=(t // tm,),
        in_specs=[pl.BlockSpec((tm, LANES), lambda i: (i, 0)), pl.BlockSpec((1, LANES), lambda i: (0, 0))],
        out_specs=pl.BlockSpec((tm, LANES), lambda i: (i, 0)),
        out_shape=jax.ShapeDtypeStruct((t, LANES), I32),
        compiler_params=_cparams(("arbitrary",)),
        name="dest_rows",
    )(sel, pstart)


def _row_copy(src_ref, src_row, dst_ref, dst_row, sem):
    return pltpu.make_async_copy(src_ref.at[pl.ds(src_row, 1)], dst_ref.at[pl.ds(dst_row, 1)], sem)


def _dispatch_kernel(dest_ref, zflag_ref, hn_ref, xr_ref, zero_sc, sem, zsem):
    tm = hn_ref.shape[0]
    n_tiles = zflag_ref.shape[0]

    def zero_copy(tile):
        start = pl.multiple_of(tile * MOE_TILE, MOE_TILE)
        return pltpu.make_async_copy(zero_sc, xr_ref.at[pl.ds(start, MOE_TILE)], zsem)

    @pl.when(pl.program_id(0) == 0)
    def _():
        zero_sc[...] = jnp.zeros_like(zero_sc)

        def start(tile, carry):
            @pl.when(zflag_ref[tile] > 0)
            def _():
                zero_copy(tile).start()
            return carry

        def wait(tile, carry):
            @pl.when(zflag_ref[tile] > 0)
            def _():
                zero_copy(tile).wait()
            return carry

        lax.fori_loop(0, n_tiles, start, 0)
        lax.fori_loop(0, n_tiles, wait, 0)

    def issue(tok, carry):
        for kk in range(TOP_K):
            _row_copy(hn_ref, tok, xr_ref, dest_ref[tok * TOP_K + kk], sem).start()
        return carry

    lax.fori_loop(0, tm, issue, 0)

    def drain(tok, carry):
        for kk in range(TOP_K):
            _row_copy(hn_ref, tok, xr_ref, dest_ref[tok * TOP_K + kk], sem).wait()
        return carry

    lax.fori_loop(0, tm, drain, 0)


def _dispatch(dest_flat, zflag, hn):
    t = hn.shape[0]
    tm = min(TM_ROWS, t)
    n_tiles = zflag.shape[0]
    return pl.pallas_call(
        _dispatch_kernel,
        grid=(t // tm,),
        in_specs=[
            pl.BlockSpec((tm * TOP_K,), lambda i: (i,), memory_space=pltpu.SMEM),
            pl.BlockSpec((n_tiles,), lambda i: (0,), memory_space=pltpu.SMEM),
            pl.BlockSpec((tm, ROW_TILES, LANES), lambda i: (i, 0, 0)),
        ],
        out_specs=pl.BlockSpec(memory_space=pl.ANY),
        out_shape=jax.ShapeDtypeStruct((n_tiles * MOE_TILE, ROW_TILES, LANES), F32),
        scratch_shapes=[pltpu.VMEM((MOE_TILE, ROW_TILES, LANES), F32), pltpu.SemaphoreType.DMA,
                        pltpu.SemaphoreType.DMA],
        compiler_params=_cparams(("arbitrary",)),
        name="dispatch",
    )(dest_flat, zflag, hn)


def _expert_kernel(te_ref, tv_ref, tf_ref, tn_ref, ts_ref, x_ref, bgu_ref, bd_ref, wgu_hbm, wd_hbm, y_ref,
                   wgu_f32, wd_f32, wgu_sc, wd_sc, sems):
    i = pl.program_id(0)

    def weight_copies(expert, slot):
        return (pltpu.make_async_copy(wgu_hbm.at[expert], wgu_f32.at[slot], sems.at[0, slot]),
                pltpu.make_async_copy(wd_hbm.at[expert], wd_f32.at[slot], sems.at[1, slot]))

    @pl.when(i == 0)
    def _():
        for copy in weight_copies(te_ref[0], ts_ref[0]):
            copy.start()

    @pl.when(tf_ref[i] > 0)
    def _():
        slot = ts_ref[i]
        for copy in weight_copies(te_ref[i], slot):
            copy.wait()

        @pl.when(tn_ref[i] >= 0)
        def _():
            for copy in weight_copies(tn_ref[i], 1 - slot):
                copy.start()

        wgu_sc[...] = wgu_f32[slot].astype(BF16)
        wd_sc[...] = wd_f32[slot].astype(BF16)

    @pl.when(tv_ref[i] > 0)
    def _():
        gu = _dot(_tiles_to_rows(x_ref[...]).astype(BF16), wgu_sc[...]) + bgu_ref[0]
        gate = jnp.minimum(gu[:, :D_EXPERT], SWIGLU_LIMIT)
        up = jnp.clip(gu[:, D_EXPERT:], -SWIGLU_LIMIT, SWIGLU_LIMIT)
        hid = (up + 1.0) * (gate * _sigmoid(SWIGLU_ALPHA * gate))
        y_ref[...] = _rows_to_tiles(_dot(hid.astype(BF16), wd_sc[...]) + bd_ref[0])

    @pl.when(tv_ref[i] == 0)
    def _():
        y_ref[...] = jnp.zeros_like(y_ref)


def _experts(tile_e, tile_valid, tile_first, tile_next, tile_slot, x_rows, wgu, bgu, wd, bd):
    n_rows = x_rows.shape[0]
    n_tiles = n_rows // MOE_TILE
    by_expert = lambda i, te, tv, tf, tn, ts: (te[i], 0, 0)
    grid_spec = pltpu.PrefetchScalarGridSpec(
        num_scalar_prefetch=5,
        grid=(n_tiles,),
        in_specs=[
            pl.BlockSpec((MOE_TILE, ROW_TILES, LANES), lambda i, te, tv, tf, tn, ts: (i, 0, 0)),
            pl.BlockSpec((1, 1, 2 * D_EXPERT), by_expert),
            pl.BlockSpec((1, 1, D_MODEL), by_expert),
            pl.BlockSpec(memory_space=pl.ANY),
            pl.BlockSpec(memory_space=pl.ANY),
        ],
        out_specs=pl.BlockSpec((MOE_TILE, ROW_TILES, LANES), lambda i, te, tv, tf, tn, ts: (i, 0, 0)),
        scratch_shapes=[
            pltpu.VMEM((2, D_MODEL, 2 * D_EXPERT), F32), pltpu.VMEM((2, D_EXPERT, D_MODEL), F32),
            pltpu.VMEM((D_MODEL, 2 * D_EXPERT), BF16), pltpu.VMEM((D_EXPERT, D_MODEL), BF16),
            pltpu.SemaphoreType.DMA((2, 2)),
        ],
    )
    return pl.pallas_call(
        _expert_kernel,
        grid_spec=grid_spec,
        out_shape=jax.ShapeDtypeStruct((n_rows, ROW_TILES, LANES), F32),
        compiler_params=_cparams(("arbitrary",)),
        name="experts",
    )(tile_e, tile_valid, tile_first, tile_next, tile_slot, x_rows, bgu, bd, wgu, wd)


def _combine_kernel(dest_ref, h_ref, gate_ref, nw_ref, yr_ref, out_ref, buf_sc, sem):
    tm = h_ref.shape[0]

    def row_copy(tok, kk):
        return _row_copy(yr_ref, dest_ref[tok * TOP_K + kk], buf_sc.at[kk], tok, sem)

    def issue(tok, carry):
        for kk in range(TOP_K):
            row_copy(tok, kk).start()
        return carry

    lax.fori_loop(0, tm, issue, 0)

    def drain(tok, carry):
        for kk in range(TOP_K):
            row_copy(tok, kk).wait()
        return carry

    lax.fori_loop(0, tm, drain, 0)

    gate = gate_ref[...]
    acc = h_ref[...]
    for kk in range(TOP_K):
        acc = acc + gate[:, kk:kk + 1] * _tiles_to_rows(buf_sc[kk])
    out_ref[...] = _rms(acc, nw_ref[...])


def _combine(dest_flat, h, gate, nw, y_rows):
    t = h.shape[0]
    tm = min(TM_ROWS, t)
    return pl.pallas_call(
        _combine_kernel,
        grid=(t // tm,),
        in_specs=[
            pl.BlockSpec((tm * TOP_K,), lambda i: (i,), memory_space=pltpu.SMEM),
            pl.BlockSpec((tm, D_MODEL), lambda i: (i, 0)),
            pl.BlockSpec((tm, LANES), lambda i: (i, 0)),
            pl.BlockSpec((1, D_MODEL), lambda i: (0, 0)),
            pl.BlockSpec(memory_space=pl.ANY),
        ],
        out_specs=pl.BlockSpec((tm, D_MODEL), lambda i: (i, 0)),
        out_shape=jax.ShapeDtypeStruct((t, D_MODEL), F32),
        scratch_shapes=[pltpu.VMEM((TOP_K, tm, ROW_TILES, LANES), F32), pltpu.SemaphoreType.DMA],
        compiler_params=_cparams(("arbitrary",)),
        name="combine",
    )(dest_flat, h, gate, nw, y_rows)


def _pad_lanes(a, offset, axis):
    n = a.shape[axis]
    pads = [(0, 0)] * a.ndim
    pads[axis] = (offset, LANES - offset - n)
    return jnp.pad(a, pads)


def _layer(x, norm_mix_w, w_in, rg_conv_w, rg_conv_b, rg_gate_a_w, rg_gate_a_b, rg_gate_x_w,
           rg_gate_x_b, rg_lambda, gdn_conv_w, gdn_a_log, gdn_dt_bias, gdn_norm_w, w_branch_rg,
           w_branch_gdn, w_out, norm_ffn_w, router_w, router_b, moe_w_gate_up, moe_b_gate_up,
           moe_w_down, moe_b_down, out_norm_w):
    t = x.shape[0]
    row = lambda a: a.reshape(1, -1).astype(F32)

    small_lo = 2 * RG_WIDTH + 2 * KEY_DIM + 2 * VAL_DIM
    small_hi = small_lo + 2 * V_HEADS
    w_z = jnp.concatenate([w_in[:, :small_lo].astype(BF16), w_in[:, small_hi:].astype(BF16)], axis=1)
    w_ba = _pad_lanes(w_in[:, small_lo:small_hi], 0, 1)
    w_ba_hi = w_ba.astype(BF16)
    w_ba_lo = (w_ba - w_ba_hi.astype(F32)).astype(BF16)
    z, ba, bat = _inproj(x, row(norm_mix_w), w_z, jnp.concatenate([w_ba_hi, w_ba_lo], axis=1))

    hg = _rg_branch(z, rg_conv_w, row(rg_conv_b), rg_gate_a_w.astype(BF16), row(rg_gate_a_b),
                    rg_gate_x_w.astype(BF16), row(rg_gate_x_b), row(rg_lambda))

    arow = jnp.stack([_pad_lanes(gdn_a_log, V_HEADS, 0), _pad_lanes(gdn_dt_bias, V_HEADS, 0)])
    o = _gdn_branch(z, gdn_conv_w, ba, bat, arow, arow.T)

    rw = _pad_lanes(router_w, 0, 1)
    rw_hi = rw.astype(BF16)
    rw = jnp.concatenate([rw_hi, (rw - rw_hi.astype(F32)).astype(BF16)], axis=1)
    rb = _pad_lanes(router_b, 0, 0).reshape(1, LANES)
    h, hn, sel, gate, counts = _merge_router(
        x, hg, o, z, w_branch_rg.astype(BF16), w_branch_gdn.astype(BF16), w_out.astype(BF16),
        row(gdn_norm_w), row(norm_ffn_w), rw, rb)

    counts = counts[0, :N_EXPERTS]
    padded = ((counts + MOE_TILE - 1) // MOE_TILE) * MOE_TILE
    pends = jnp.cumsum(padded)
    pstarts = pends - padded
    n_tiles = (t * TOP_K) // MOE_TILE + N_EXPERTS
    tile_start = jnp.arange(n_tiles, dtype=I32) * MOE_TILE
    tile_valid = (tile_start < pends[-1]).astype(I32)
    tile_e = jnp.minimum(jnp.sum(pends[None, :] <= tile_start[:, None], axis=1), N_EXPERTS - 1).astype(I32)
    prev_e = jnp.concatenate([jnp.full((1,), -1, I32), tile_e[:-1]])
    tile_first = jnp.logical_and(tile_e != prev_e, tile_valid > 0).astype(I32)
    tile_slot = ((jnp.cumsum(tile_first) - 1) % 2).astype(I32)
    first_pos = jnp.where(tile_first > 0, jnp.arange(n_tiles, dtype=I32), n_tiles)
    next_first = jnp.concatenate([lax.cummin(first_pos[::-1])[::-1][1:], jnp.full((1,), n_tiles, I32)])
    tile_next = jnp.where(next_first < n_tiles, tile_e[jnp.minimum(next_first, n_tiles - 1)], -1).astype(I32)
    group_end = jnp.any(jnp.logical_and((tile_start + MOE_TILE)[:, None] == pends[None, :],
                                        padded[None, :] > 0), axis=1)
    zflag = jnp.logical_or(group_end, tile_valid == 0).astype(I32)

    dest = _dest_rows(sel, _pad_lanes(pstarts.astype(I32), 0, 0).reshape(1, LANES))
    dest_flat = dest[:, :TOP_K].reshape(t * TOP_K)

    x_rows = _dispatch(dest_flat, zflag, hn)
    y_rows = _experts(tile_e, tile_valid, tile_first, tile_next, tile_slot, x_rows, moe_w_gate_up,
                      moe_b_gate_up.reshape(N_EXPERTS, 1, 2 * D_EXPERT), moe_w_down,
                      moe_b_down.reshape(N_EXPERTS, 1, D_MODEL))

    return _combine(dest_flat, h, gate, row(out_norm_w), y_rows)


def kernel(x, norm_mix_w, w_in, rg_conv_w, rg_conv_b, rg_gate_a_w, rg_gate_a_b, rg_gate_x_w,
           rg_gate_x_b, rg_lambda, gdn_conv_w, gdn_A_log, gdn_dt_bias, gdn_norm_w, w_branch_rg,
           w_branch_gdn, w_out, norm_ffn_w, router_w, router_b, moe_w_gate_up, moe_b_gate_up,
           moe_w_down, moe_b_down, norm_final_w):
    assert x.shape[0] == 1 and norm_mix_w.shape[0] == 1
    out = _layer(x[0], norm_mix_w[0], w_in[0], rg_conv_w[0], rg_conv_b[0], rg_gate_a_w[0],
                 rg_gate_a_b[0], rg_gate_x_w[0], rg_gate_x_b[0], rg_lambda[0], gdn_conv_w[0],
                 gdn_A_log[0], gdn_dt_bias[0], gdn_norm_w[0], w_branch_rg[0], w_branch_gdn[0],
                 w_out[0], norm_ffn_w[0], router_w[0], router_b[0], moe_w_gate_up[0],
                 moe_b_gate_up[0], moe_w_down[0], moe_b_down[0], norm_final_w)
    return out[None]
```

```python
import functools

import jax
import jax.numpy as jnp
from jax import lax
from jax.experimental import pallas as pl
from jax.experimental.pallas import tpu as pltpu

F32 = jnp.float32
BF16 = jnp.bfloat16
I32 = jnp.int32
HI = lax.Precision.HIGHEST

D_MODEL = 1024
RG_WIDTH = 1024
RG_BLOCKS = 4
RG_BLOCK_DIM = RG_WIDTH // RG_BLOCKS
RG_C = 8.0
CONV_K = 4
QK_HEADS = 8
V_HEADS = 16
HEAD_DIM = 128
KEY_DIM = QK_HEADS * HEAD_DIM
VAL_DIM = V_HEADS * HEAD_DIM
CHUNK = 64
N_EXPERTS = 32
TOP_K = 4
D_EXPERT = 1024
SWIGLU_ALPHA = 1.702
SWIGLU_LIMIT = 7.0
EPS = 1e-6

LANES = 128
SUBLANES = 8
VMEM_LIMIT = 56 * 1024 * 1024

Z_RGX, Z_RGY, Z_Q, Z_K = 0, 1, 2, 3
Z_V, Z_ZG = 2, 3
Z_GA, Z_GB = 8, 9
Z_WIDTH = 10 * 1024

TM_INPROJ = 1024
TN_INPROJ = 2048
TM_RG = 256
GDN_NCH = 2
TM_MERGE = 512
TM_ROWS = 512
MOE_TILE = 512


def _cparams(sem):
    return pltpu.CompilerParams(dimension_semantics=sem, vmem_limit_bytes=VMEM_LIMIT)


def _softplus(x):
    return jnp.maximum(x, 0.0) + jnp.log1p(jnp.exp(-jnp.abs(x)))


def _sigmoid(x):
    return 1.0 / (1.0 + jnp.exp(-x))


def _silu(x):
    return x * _sigmoid(x)


def _gelu_tanh(x):
    c = 0.7978845608028654
    return 0.5 * x * (1.0 + jnp.tanh(c * (x + 0.044715 * (x * x * x))))


def _rms(x, w):
    return x * lax.rsqrt(jnp.mean(x * x, axis=-1, keepdims=True) + EPS) * w


def _dot(a, b):
    return jnp.dot(a, b, preferred_element_type=F32)


ROW_TILES = D_MODEL // LANES


def _rows_to_tiles(x):
    parts = jnp.stack([x[:, s * LANES:(s + 1) * LANES] for s in range(ROW_TILES)], axis=0)
    return pltpu.einshape("snl->nsl", parts)


def _tiles_to_rows(x):
    parts = pltpu.einshape("nsl->snl", x)
    return jnp.concatenate([parts[s] for s in range(ROW_TILES)], axis=1)


def _dot_hi(a, b):
    return jnp.dot(a, b, precision=HI, preferred_element_type=F32)


def _bf16_terms(x):
    t0 = x.astype(BF16)
    r = x - t0.astype(F32)
    t1 = r.astype(BF16)
    t2 = (r - t1.astype(F32)).astype(BF16)
    return t0, t1, t2


def _dot_nt(a, b, precision=None):
    return lax.dot_general(a, b, (((1,), (1,)), ((), ())), precision=precision,
                           preferred_element_type=F32)


def _dot_tn(a, b):
    return lax.dot_general(a, b, (((0,), (0,)), ((), ())), preferred_element_type=F32)


def _inproj_kernel(x_ref, nw_ref, wh_ref, wt_ref, wba_ref, z_ref, ba_ref, bat_ref, xn_sc, *, n_head):
    j = pl.program_id(1)

    @pl.when(j == 0)
    def _():
        xn = _rms(x_ref[...], nw_ref[...])
        x_hi = xn.astype(BF16)
        xn_sc[...] = x_hi
        x_lo = (xn - x_hi.astype(F32)).astype(BF16)
        wba = wba_ref[...]
        hi = _dot(x_hi, wba)
        ba = hi[:, :LANES] + hi[:, LANES:] + _dot(x_lo, wba[:, :LANES])
        ba_ref[...] = ba
        bat_ref[...] = ba.T

    @pl.when(j < n_head)
    def _():
        z_ref[...] = _dot(xn_sc[...], wh_ref[...]).astype(BF16)

    @pl.when(j >= n_head)
    def _():
        z_ref[...] = _dot(xn_sc[...], wt_ref[...]).astype(BF16)


def _inproj(x, nw, w_head, w_tail, w_ba):
    t = x.shape[0]
    tm, tn = min(TM_INPROJ, t), TN_INPROJ
    n_head = w_head.shape[1] // tn
    assert w_head.shape[1] % tn == 0 and w_tail.shape[1] % tn == 0
    return pl.pallas_call(
        functools.partial(_inproj_kernel, n_head=n_head),
        grid=(t // tm, Z_WIDTH // tn),
        in_specs=[
            pl.BlockSpec((tm, D_MODEL), lambda i, j: (i, 0)),
            pl.BlockSpec((1, D_MODEL), lambda i, j: (0, 0)),
            pl.BlockSpec((D_MODEL, tn), lambda i, j: (0, jnp.minimum(j, n_head - 1))),
            pl.BlockSpec((D_MODEL, tn), lambda i, j: (0, jnp.maximum(j - n_head, 0))),
            pl.BlockSpec((D_MODEL, 2 * LANES), lambda i, j: (0, 0)),
        ],
        out_specs=[
            pl.BlockSpec((tm, tn), lambda i, j: (i, j)),
            pl.BlockSpec((tm, LANES), lambda i, j: (i, 0)),
            pl.BlockSpec((LANES, tm), lambda i, j: (0, i)),
        ],
        out_shape=[
            jax.ShapeDtypeStruct((t, Z_WIDTH), BF16),
            jax.ShapeDtypeStruct((t, LANES), F32),
            jax.ShapeDtypeStruct((LANES, t), F32),
        ],
        scratch_shapes=[pltpu.VMEM((tm, D_MODEL), BF16)],
        compiler_params=_cparams(("arbitrary", "arbitrary")),
        name="inproj",
    )(x, nw, w_head, w_tail, w_ba)


def _delay_matrix(tm):
    t = jnp.arange(tm)
    return jnp.concatenate([(t[None, :] == t[:, None] - s) for s in range(1, CONV_K)], axis=0).astype(BF16)


def _causal_conv(x_ref, hist_sc, cw_ref, delay_ref, lo):
    tm, width = x_ref.shape
    cols = slice(lo, lo + width)
    xb = x_ref[...]
    x = xb.astype(F32)
    delayed = _dot(delay_ref[...], xb)
    acc = cw_ref[CONV_K - 1:CONV_K, cols] * x
    prev = hist_sc[:, cols]
    row = lax.broadcasted_iota(I32, (SUBLANES, 1), 0)
    head = jnp.zeros((SUBLANES, width), F32)
    for s in range(1, CONV_K):
        tap = cw_ref[CONV_K - 1 - s:CONV_K - s, cols]
        acc = acc + tap * delayed[(s - 1) * tm:s * tm, :]
        head = head + tap * jnp.where(row < s, pltpu.roll(prev, s, axis=0), 0.0)
    hist_sc[:, cols] = x[tm - SUBLANES:, :]
    return jnp.concatenate([acc[:SUBLANES, :] + head, acc[SUBLANES:, :]], axis=0)


def _rg_kernel(x_ref, y_ref, cw_ref, cb_ref, wa_ref, ba_ref, wx_ref, bx_ref, lam_ref, delay_ref,
               out_ref, prev_sc, h_sc):
    i = pl.program_id(0)
    tm = x_ref.shape[0]

    @pl.when(i == 0)
    def _():
        prev_sc[...] = jnp.zeros_like(prev_sc)
        h_sc[...] = jnp.zeros_like(h_sc)

    xa = _causal_conv(x_ref, prev_sc, cw_ref, delay_ref, 0) + cb_ref[...]

    xab = xa.astype(BF16)
    r_parts, i_parts = [], []
    for blk in range(RG_BLOCKS):
        xs = xab[:, blk * RG_BLOCK_DIM:(blk + 1) * RG_BLOCK_DIM]
        r_parts.append(_dot(xs, wa_ref[blk]))
        i_parts.append(_dot(xs, wx_ref[blk]))
    r = _sigmoid(jnp.concatenate(r_parts, axis=1) + ba_ref[...])
    ig = _sigmoid(jnp.concatenate(i_parts, axis=1) + bx_ref[...])

    log_a = (-RG_C) * r * _softplus(-lam_ref[...])
    a = jnp.exp(log_a)
    m2 = jnp.maximum(1.0 - a * a, 0.0)
    mult = jnp.where(m2 > 0.0, m2 * lax.rsqrt(m2), 0.0)
    rows = lax.broadcasted_iota(I32, (tm, 1), 0)
    mult = jnp.where(jnp.logical_and(rows == 0, i == 0), 1.0, mult)
    b = mult * (ig * xa)

    groups = tm // SUBLANES
    a = a.reshape(groups, SUBLANES, RG_WIDTH)
    b = b.reshape(groups, SUBLANES, RG_WIDTH)
    sub = lax.broadcasted_iota(I32, (1, SUBLANES, 1), 1)
    s = 1
    while s < SUBLANES:
        keep = sub >= s
        a_sh = pltpu.roll(a, s, axis=1)
        b_sh = pltpu.roll(b, s, axis=1)
        b = b + jnp.where(keep, a * b_sh, 0.0)
        a = jnp.where(keep, a * a_sh, a)
        s *= 2
    h_prev = h_sc[...]
    hs = []
    for g in range(groups):
        hs.append(b[g] + a[g] * h_prev)
        h_prev = hs[-1][SUBLANES - 1:SUBLANES, :]
    h_sc[...] = h_prev
    h = jnp.concatenate(hs, axis=0)
    out_ref[...] = (h * _gelu_tanh(y_ref[...].astype(F32))).astype(BF16)


def _rg_branch(z, cw, cb, wa, ba, wx, bx, lam):
    t = z.shape[0]
    tm = min(TM_RG, t)
    full = lambda shape: pl.BlockSpec(shape, lambda i: (0,) * len(shape))
    return pl.pallas_call(
        _rg_kernel,
        grid=(t // tm,),
        in_specs=[
            pl.BlockSpec((tm, RG_WIDTH), lambda i: (i, Z_RGX)),
            pl.BlockSpec((tm, RG_WIDTH), lambda i: (i, Z_RGY)),
            full((CONV_K, RG_WIDTH)),
            full((1, RG_WIDTH)),
            full((RG_BLOCKS, RG_BLOCK_DIM, RG_BLOCK_DIM)),
            full((1, RG_WIDTH)),
            full((RG_BLOCKS, RG_BLOCK_DIM, RG_BLOCK_DIM)),
            full((1, RG_WIDTH)),
            full((1, RG_WIDTH)),
            full(((CONV_K - 1) * tm, tm)),
        ],
        out_specs=pl.BlockSpec((tm, RG_WIDTH), lambda i: (i, 0)),
        out_shape=jax.ShapeDtypeStruct((t, RG_WIDTH), BF16),
        scratch_shapes=[pltpu.VMEM((SUBLANES, RG_WIDTH), F32), pltpu.VMEM((1, RG_WIDTH), F32)],
        compiler_params=_cparams(("arbitrary",)),
        name="rg_branch",
    )(z, z, cw, cb, wa, ba, wx, bx, lam, _delay_matrix(tm))


def _l2norm_heads(x, n_heads):
    parts = []
    for h in range(n_heads):
        xh = x[:, h * HEAD_DIM:(h + 1) * HEAD_DIM]
        parts.append(xh * lax.rsqrt(jnp.sum(xh * xh, axis=-1, keepdims=True) + EPS))
    return jnp.concatenate(parts, axis=1)


def _conv_silu(x_ref, prev_sc, cw_ref, delay_ref, lo):
    return _silu(_causal_conv(x_ref, prev_sc, cw_ref, delay_ref, lo))


def _unit_lower_inverses(mats):
    c = mats[0].shape[0]
    ri = lax.broadcasted_iota(I32, (c, c), 0)
    ci = lax.broadcasted_iota(I32, (c, c), 1)
    eye = jnp.where(ri == ci, 1.0, 0.0)
    ps = [eye - a for a in mats]
    xbs = [a.astype(BF16) for a in mats]
    xbs = [_dot(xb, xb).astype(BF16) for xb in xbs]
    power = 2
    while power < c:
        ps = [p + _dot(p.astype(BF16), xb) for p, xb in zip(ps, xbs)]
        power *= 2
        if power < c:
            xbs = [_dot(xb, xb).astype(BF16) for xb in xbs]
    return ps


def _gdn_kernel(q_ref, k_ref, v_ref, cw_ref, ba_ref, bat_ref, arow_ref, acol_ref, msum_ref, delay_ref,
                o_ref, s_sc, gt_sc, prev_sc, *, nch):
    hg_n = V_HEADS
    c = CHUNK
    tile = nch * c
    scale = HEAD_DIM ** -0.5

    @pl.when(pl.program_id(0) == 0)
    def _():
        s_sc[...] = jnp.zeros_like(s_sc)
        prev_sc[...] = jnp.zeros_like(prev_sc)

    q_all = _l2norm_heads(_conv_silu(q_ref, prev_sc, cw_ref, delay_ref, 0), QK_HEADS)
    k_all = _l2norm_heads(_conv_silu(k_ref, prev_sc, cw_ref, delay_ref, KEY_DIM), QK_HEADS)
    v_all = _conv_silu(v_ref, prev_sc, cw_ref, delay_ref, 2 * KEY_DIM)
    q_bf, k_bf = q_all.astype(BF16), k_all.astype(BF16)

    ba = ba_ref[...]
    arow = arow_ref[...]
    g_all = -jnp.exp(arow[0:1, :]) * _softplus(ba + arow[1:2, :])
    beta_all = _sigmoid(ba)
    msum = msum_ref[...]
    sums = sum(_dot(msum, term) for term in _bf16_terms(g_all))
    gcs = sums[:tile, :]
    gls = sums[tile:, :]
    egc = jnp.exp(gcs)
    ekd = jnp.exp(gls - gcs)
    acol = acol_ref[...]
    g_t = -jnp.exp(acol[:, 0:1]) * _softplus(bat_ref[...] + acol[:, 1:2])
    gt_sc[...] = sum(_dot_nt(term, msum[:tile, :]) for term in _bf16_terms(g_t))

    lane = lax.broadcasted_iota(I32, (tile, LANES), 1)
    ri = lax.broadcasted_iota(I32, (c, c), 0)
    ci = lax.broadcasted_iota(I32, (c, c), 1)
    causal = ri >= ci
    strict = ri > ci

    def column(arr, lane_idx):
        return jnp.sum(jnp.where(lane == lane_idx, arr, 0.0), axis=1, keepdims=True)

    cols = []
    for j in range(hg_n):
        head = j
        b_col = column(beta_all, head)
        egc_col = column(egc, V_HEADS + head)
        cols.append(dict(
            b=b_col, gcs=column(gcs, V_HEADS + head), egc=egc_col,
            ekd=column(ekd, V_HEADS + head), be=b_col * egc_col,
            g_row=gt_sc[pl.ds(V_HEADS + head, 1), :]))

    pairs = [(n, j) for n in range(nch) for j in range(hg_n)]
    rows_of = lambda n: slice(n * c, (n + 1) * c)
    head_cols = lambda j: slice(j * HEAD_DIM, (j + 1) * HEAD_DIM)
    kq = {}
    for n in range(nch):
        for qh in range(hg_n // 2):
            k = k_bf[rows_of(n), head_cols(qh)]
            q = q_bf[rows_of(n), head_cols(qh)]
            kq[(n, qh)] = dict(kk=_dot_nt(k, k), qk=_dot_nt(q, k), kf=k_all[rows_of(n), head_cols(qh)],
                               qf=q_all[rows_of(n), head_cols(qh)])
    decays, a_mats = {}, []
    for n, j in pairs:
        col = cols[j]
        diff = col["gcs"][rows_of(n), :] - col["g_row"][:, rows_of(n)]
        decay = jnp.where(causal, jnp.exp(jnp.where(causal, diff, 0.0)), 0.0)
        decays[(n, j)] = decay
        a_mats.append(jnp.where(strict, col["b"][rows_of(n), :] * kq[(n, j // 2)]["kk"] * decay, 0.0))
    t_mats = _unit_lower_inverses(a_mats)
    uws, attns, q_decs, k_decs = {}, {}, {}, {}
    for (n, j), t_mat in zip(pairs, t_mats):
        col, kqn = cols[j], kq[(n, j // 2)]
        v = v_all[rows_of(n), head_cols(j)]
        rhs = jnp.concatenate([v * col["b"][rows_of(n), :], kqn["kf"] * col["be"][rows_of(n), :]], axis=1)
        uws[(n, j)] = _dot(t_mat.astype(BF16), rhs.astype(BF16))
        attns[(n, j)] = (kqn["qk"] * decays[(n, j)]).astype(BF16)
        q_decs[(n, j)] = kqn["qf"] * col["egc"][rows_of(n), :]
        k_decs[(n, j)] = (kqn["kf"] * col["ekd"][rows_of(n), :]).astype(BF16)

    states = [s_sc[j] for j in range(hg_n)]
    for n in range(nch):
        heads = range(hg_n)
        ws_qs = [_dot(jnp.concatenate([uws[(n, j)][:, HEAD_DIM:], q_decs[(n, j)]], axis=0).astype(BF16),
                      states[j].astype(BF16)) for j in heads]
        v_news = [(uws[(n, j)][:, :HEAD_DIM] - ws_qs[j][:c, :]).astype(BF16) for j in heads]
        outs = [(ws_qs[j][c:, :] + _dot(attns[(n, j)], v_news[j])) * scale for j in heads]
        for j in heads:
            o_ref[rows_of(n), head_cols(j)] = outs[j].astype(BF16)
        egls = [jnp.exp(cols[j]["g_row"][:, (n + 1) * c - 1:(n + 1) * c]) for j in heads]
        states = [states[j] * egls[j] + _dot_tn(k_decs[(n, j)], v_news[j]) for j in heads]
    for j in range(hg_n):
        s_sc[j] = states[j]


def _gdn_branch(z, cw, ba, bat, arow, acol):
    t = z.shape[0]
    nch = GDN_NCH
    tile = nch * CHUNK
    idx = jnp.arange(tile)
    same = (idx[:, None] // CHUNK) == (idx[None, :] // CHUNK)
    mtri = jnp.logical_and(same, idx[:, None] >= idx[None, :])
    msum = jnp.concatenate([mtri, same], axis=0).astype(BF16)
    return pl.pallas_call(
        functools.partial(_gdn_kernel, nch=nch),
        grid=(t // tile,),
        in_specs=[
            pl.BlockSpec((tile, KEY_DIM), lambda n: (n, Z_Q)),
            pl.BlockSpec((tile, KEY_DIM), lambda n: (n, Z_K)),
            pl.BlockSpec((tile, VAL_DIM), lambda n: (n, Z_V)),
            pl.BlockSpec((CONV_K, 2 * KEY_DIM + VAL_DIM), lambda n: (0, 0)),
            pl.BlockSpec((tile, LANES), lambda n: (n, 0)),
            pl.BlockSpec((LANES, tile), lambda n: (0, n)),
            pl.BlockSpec((2, LANES), lambda n: (0, 0)),
            pl.BlockSpec((LANES, 2), lambda n: (0, 0)),
            pl.BlockSpec((2 * tile, tile), lambda n: (0, 0)),
            pl.BlockSpec(((CONV_K - 1) * tile, tile), lambda n: (0, 0)),
        ],
        out_specs=pl.BlockSpec((tile, VAL_DIM), lambda n: (n, 0)),
        out_shape=jax.ShapeDtypeStruct((t, VAL_DIM), BF16),
        scratch_shapes=[pltpu.VMEM((V_HEADS, HEAD_DIM, HEAD_DIM), F32), pltpu.VMEM((LANES, tile), F32),
                        pltpu.VMEM((SUBLANES, 2 * KEY_DIM + VAL_DIM), F32)],
        compiler_params=_cparams(("arbitrary",)),
        name="gdn_branch",
    )(z, z, z, cw, ba, bat, arow, acol, msum, _delay_matrix(tile))


def _merge_kernel(x_ref, hg_ref, o_ref, zg_ref, ga_ref, gb_ref, wrg_ref, wgdn_ref, wout_ref,
                  gnw_ref, fnw_ref, rw_ref, rb_ref, lstrict_ref,
                  h_ref, hn_ref, sel_ref, gate_ref, cnt_ref, carry_sc):
    tm = x_ref.shape[0]

    @pl.when(pl.program_id(0) == 0)
    def _():
        carry_sc[...] = jnp.zeros_like(carry_sc)

    ya = _dot(hg_ref[...], wrg_ref[...])
    o = o_ref[...].astype(F32)
    zg = zg_ref[...].astype(F32)
    gnw = gnw_ref[...]
    parts = []
    for h in range(V_HEADS):
        sl = slice(h * HEAD_DIM, (h + 1) * HEAD_DIM)
        parts.append((_rms(o[:, sl], gnw) * _silu(zg[:, sl])).astype(BF16))
    yb = _dot(jnp.concatenate(parts, axis=1), wgdn_ref[...])
    mix = _sigmoid(ga_ref[...].astype(F32)) * ya + _sigmoid(gb_ref[...].astype(F32)) * yb
    h = x_ref[...] + _dot(mix.astype(BF16), wout_ref[...])
    h_ref[...] = h

    hn = _rms(h, fnw_ref[...])
    hn_ref[...] = _rows_to_tiles(hn)
    lane = lax.broadcasted_iota(I32, (tm, LANES), 1)
    lane_f = lane.astype(F32)
    neg_inf = float("-inf")
    hn_hi = hn.astype(BF16)
    hn_lo = (hn - hn_hi.astype(F32)).astype(BF16)
    rw = rw_ref[...]
    hi = _dot(hn_hi, rw)
    raw = hi[:, :LANES] + hi[:, LANES:] + _dot(hn_lo, rw[:, :LANES])
    logits = jnp.where(lane < N_EXPERTS, raw + rb_ref[...], neg_inf)
    vals, idxs = [], []
    multi_hot = jnp.zeros((tm, LANES), F32)
    for _ in range(TOP_K):
        m = jnp.max(logits, axis=-1, keepdims=True)
        first = jnp.min(jnp.where(logits == m, lane_f, float(LANES)), axis=-1, keepdims=True)
        hit = lane_f == first
        multi_hot = multi_hot + jnp.where(hit, 1.0, 0.0)
        logits = jnp.where(hit, neg_inf, logits)
        vals.append(m)
        idxs.append(first)
    exps = [jnp.exp(v - vals[0]) for v in vals]
    denom = exps[0] + exps[1] + exps[2] + exps[3]

    before = _dot(lstrict_ref[...], multi_hot.astype(BF16)) + carry_sc[...]
    carry_sc[...] = carry_sc[...] + jnp.sum(multi_hot, axis=0, keepdims=True)
    cnt_ref[...] = carry_sc[...].astype(I32)

    sel = jnp.zeros((tm, LANES), F32)
    gate = jnp.zeros((tm, LANES), F32)
    for kk in range(TOP_K):
        rank = jnp.sum(jnp.where(lane_f == idxs[kk], before, 0.0), axis=-1, keepdims=True)
        sel = jnp.where(lane == kk, idxs[kk], sel)
        sel = jnp.where(lane == TOP_K + kk, rank, sel)
        gate = jnp.where(lane == kk, exps[kk] / denom, gate)
    sel_ref[...] = sel.astype(I32)
    gate_ref[...] = gate


def _merge_router(x, hg, o, z, wrg, wgdn, wout, gnw, fnw, rw, rb):
    t = x.shape[0]
    tm = min(TM_MERGE, t)
    idx = jnp.arange(tm)
    lstrict = (idx[:, None] > idx[None, :]).astype(BF16)
    full = lambda shape: pl.BlockSpec(shape, lambda i: (0,) * len(shape))
    row = lambda width, col=0: pl.BlockSpec((tm, width), lambda i: (i, col))
    return pl.pallas_call(
        _merge_kernel,
        grid=(t // tm,),
        in_specs=[
            row(D_MODEL), row(RG_WIDTH), row(VAL_DIM), row(VAL_DIM, Z_ZG), row(D_MODEL, Z_GA),
            row(D_MODEL, Z_GB),
            full((RG_WIDTH, D_MODEL)), full((VAL_DIM, D_MODEL)), full((D_MODEL, D_MODEL)),
            full((1, HEAD_DIM)), full((1, D_MODEL)), full((D_MODEL, 2 * LANES)), full((1, LANES)),
            full((tm, tm)),
        ],
        out_specs=[row(D_MODEL), pl.BlockSpec((tm, ROW_TILES, LANES), lambda i: (i, 0, 0)), row(LANES),
                   row(LANES), full((1, LANES))],
        out_shape=[
            jax.ShapeDtypeStruct((t, D_MODEL), F32),
            jax.ShapeDtypeStruct((t, ROW_TILES, LANES), F32),
            jax.ShapeDtypeStruct((t, LANES), I32),
            jax.ShapeDtypeStruct((t, LANES), F32),
            jax.ShapeDtypeStruct((1, LANES), I32),
        ],
        scratch_shapes=[pltpu.VMEM((1, LANES), F32)],
        compiler_params=_cparams(("arbitrary",)),
        name="merge_router",
    )(x, hg, o, z, z, z, wrg, wgdn, wout, gnw, fnw, rw, rb, lstrict)


def _dest_kernel(sel_ref, pstart_ref, dest_ref):
    tm = sel_ref.shape[0]
    sel = sel_ref[...].astype(F32)
    lane = lax.broadcasted_iota(I32, (tm, LANES), 1)
    lane_f = lane.astype(F32)
    pstart = pstart_ref[...].astype(F32)
    dest = jnp.zeros((tm, LANES), F32)
    for kk in range(TOP_K):
        e = jnp.sum(jnp.where(lane == kk, sel, 0.0), axis=-1, keepdims=True)
        rank = jnp.sum(jnp.where(lane == TOP_K + kk, sel, 0.0), axis=-1, keepdims=True)
        base = jnp.sum(jnp.where(lane_f == e, pstart, 0.0), axis=-1, keepdims=True)
        dest = jnp.where(lane == kk, base + rank, dest)
    dest_ref[...] = dest.astype(I32)


def _dest_rows(sel, pstart):
    t = sel.shape[0]
    tm = min(1024, t)
    return pl.pallas_call(
        _dest_kernel,
        grid=(t // tm,),
        in_specs=[pl.BlockSpec((tm, LANES), lambda i: (i, 0)), pl.BlockSpec((1, LANES), lambda i: (0, 0))],
        out_specs=pl.BlockSpec((tm, LANES), lambda i: (i, 0)),
        out_shape=jax.ShapeDtypeStruct((t, LANES), I32),
        compiler_params=_cparams(("arbitrary",)),
        name="dest_rows",
    )(sel, pstart)


def _row_copy(src_ref, src_row, dst_ref, dst_row, sem):
    return pltpu.make_async_copy(src_ref.at[pl.ds(src_row, 1)], dst_ref.at[pl.ds(dst_row, 1)], sem)


def _dispatch_kernel(dest_ref, zflag_ref, hn_ref, xr_ref, zero_sc, sem, zsem):
    tm = hn_ref.shape[0]
    n_tiles = zflag_ref.shape[0]

    def zero_copy(tile):
        start = pl.multiple_of(tile * MOE_TILE, MOE_TILE)
        return pltpu.make_async_copy(zero_sc, xr_ref.at[pl.ds(start, MOE_TILE)], zsem)

    @pl.when(pl.program_id(0) == 0)
    def _():
        zero_sc[...] = jnp.zeros_like(zero_sc)

        def start(tile, carry):
            @pl.when(zflag_ref[tile] > 0)
            def _():
                zero_copy(tile).start()
            return carry

        def wait(tile, carry):
            @pl.when(zflag_ref[tile] > 0)
            def _():
                zero_copy(tile).wait()
            return carry

        lax.fori_loop(0, n_tiles, start, 0)
        lax.fori_loop(0, n_tiles, wait, 0)

    def issue(tok, carry):
        for kk in range(TOP_K):
            _row_copy(hn_ref, tok, xr_ref, dest_ref[tok * TOP_K + kk], sem).start()
        return carry

    lax.fori_loop(0, tm, issue, 0)

    def drain(tok, carry):
        for kk in range(TOP_K):
            _row_copy(hn_ref, tok, xr_ref, dest_ref[tok * TOP_K + kk], sem).wait()
        return carry

    lax.fori_loop(0, tm, drain, 0)


def _dispatch(dest_flat, zflag, hn):
    t = hn.shape[0]
    tm = min(TM_ROWS, t)
    n_tiles = zflag.shape[0]
    return pl.pallas_call(
        _dispatch_kernel,
        grid=(t // tm,),
        in_specs=[
            pl.BlockSpec((tm * TOP_K,), lambda i: (i,), memory_space=pltpu.SMEM),
            pl.BlockSpec((n_tiles,), lambda i: (0,), memory_space=pltpu.SMEM),
            pl.BlockSpec((tm, ROW_TILES, LANES), lambda i: (i, 0, 0)),
        ],
        out_specs=pl.BlockSpec(memory_space=pl.ANY),
        out_shape=jax.ShapeDtypeStruct((n_tiles * MOE_TILE, ROW_TILES, LANES), F32),
        scratch_shapes=[pltpu.VMEM((MOE_TILE, ROW_TILES, LANES), F32), pltpu.SemaphoreType.DMA,
                        pltpu.SemaphoreType.DMA],
        compiler_params=_cparams(("arbitrary",)),
        name="dispatch",
    )(dest_flat, zflag, hn)


def _expert_kernel(te_ref, tv_ref, tf_ref, tn_ref, ts_ref, x_ref, bgu_ref, bd_ref, wgu_hbm, wd_hbm, y_ref,
                   wgu_f32, wd_f32, wgu_sc, wd_sc, sems):
    i = pl.program_id(0)

    def weight_copies(expert, slot):
        return (pltpu.make_async_copy(wgu_hbm.at[expert], wgu_f32.at[slot], sems.at[0, slot]),
                pltpu.make_async_copy(wd_hbm.at[expert], wd_f32.at[slot], sems.at[1, slot]))

    @pl.when(i == 0)
    def _():
        for copy in weight_copies(te_ref[0], ts_ref[0]):
            copy.start()

    @pl.when(tf_ref[i] > 0)
    def _():
        slot = ts_ref[i]
        for copy in weight_copies(te_ref[i], slot):
            copy.wait()

        @pl.when(tn_ref[i] >= 0)
        def _():
            for copy in weight_copies(tn_ref[i], 1 - slot):
                copy.start()

        wgu_sc[...] = wgu_f32[slot].astype(BF16)
        wd_sc[...] = wd_f32[slot].astype(BF16)

    @pl.when(tv_ref[i] > 0)
    def _():
        gu = _dot(_tiles_to_rows(x_ref[...]).astype(BF16), wgu_sc[...]) + bgu_ref[0]
        gate = jnp.minimum(gu[:, :D_EXPERT], SWIGLU_LIMIT)
        up = jnp.clip(gu[:, D_EXPERT:], -SWIGLU_LIMIT, SWIGLU_LIMIT)
        hid = (up + 1.0) * (gate * _sigmoid(SWIGLU_ALPHA * gate))
        y_ref[...] = _rows_to_tiles(_dot(hid.astype(BF16), wd_sc[...]) + bd_ref[0])

    @pl.when(tv_ref[i] == 0)
    def _():
        y_ref[...] = jnp.zeros_like(y_ref)


def _experts(tile_e, tile_valid, tile_first, tile_next, tile_slot, x_rows, wgu, bgu, wd, bd):
    n_rows = x_rows.shape[0]
    n_tiles = n_rows // MOE_TILE
    by_expert = lambda i, te, tv, tf, tn, ts: (te[i], 0, 0)
    grid_spec = pltpu.PrefetchScalarGridSpec(
        num_scalar_prefetch=5,
        grid=(n_tiles,),
        in_specs=[
            pl.BlockSpec((MOE_TILE, ROW_TILES, LANES), lambda i, te, tv, tf, tn, ts: (i, 0, 0)),
            pl.BlockSpec((1, 1, 2 * D_EXPERT), by_expert),
            pl.BlockSpec((1, 1, D_MODEL), by_expert),
            pl.BlockSpec(memory_space=pl.ANY),
            pl.BlockSpec(memory_space=pl.ANY),
        ],
        out_specs=pl.BlockSpec((MOE_TILE, ROW_TILES, LANES), lambda i, te, tv, tf, tn, ts: (i, 0, 0)),
        scratch_shapes=[
            pltpu.VMEM((2, D_MODEL, 2 * D_EXPERT), F32), pltpu.VMEM((2, D_EXPERT, D_MODEL), F32),
            pltpu.VMEM((D_MODEL, 2 * D_EXPERT), BF16), pltpu.VMEM((D_EXPERT, D_MODEL), BF16),
            pltpu.SemaphoreType.DMA((2, 2)),
        ],
    )
    return pl.pallas_call(
        _expert_kernel,
        grid_spec=grid_spec,
        out_shape=jax.ShapeDtypeStruct((n_rows, ROW_TILES, LANES), F32),
        compiler_params=_cparams(("arbitrary",)),
        name="experts",
    )(tile_e, tile_valid, tile_first, tile_next, tile_slot, x_rows, bgu, bd, wgu, wd)


def _combine_kernel(dest_ref, dnext_ref, h_ref, gate_ref, nw_ref, yr_ref, out_ref, buf_sc, sems):
    i = pl.program_id(0)
    tm = h_ref.shape[0]
    slot = i % 2

    def row_copy(d_ref, s, tok, kk):
        return _row_copy(yr_ref, d_ref[tok * TOP_K + kk], buf_sc.at[s, kk], tok, sems.at[s])

    def issue(d_ref, s):
        def body(tok, carry):
            for kk in range(TOP_K):
                row_copy(d_ref, s, tok, kk).start()
            return carry

        lax.fori_loop(0, tm, body, 0)

    @pl.when(i == 0)
    def _():
        issue(dest_ref, 0)

    @pl.when(i + 1 < pl.num_programs(0))
    def _():
        issue(dnext_ref, 1 - slot)

    def drain(tok, carry):
        for kk in range(TOP_K):
            row_copy(dest_ref, slot, tok, kk).wait()
        return carry

    lax.fori_loop(0, tm, drain, 0)

    gate = gate_ref[...]
    acc = h_ref[...]
    for kk in range(TOP_K):
        acc = acc + gate[:, kk:kk + 1] * _tiles_to_rows(buf_sc[slot, kk])
    out_ref[...] = _rms(acc, nw_ref[...])


def _combine(dest_flat, h, gate, nw, y_rows):
    t = h.shape[0]
    tm = min(TM_ROWS, t)
    steps = t // tm
    return pl.pallas_call(
        _combine_kernel,
        grid=(steps,),
        in_specs=[
            pl.BlockSpec((tm * TOP_K,), lambda i: (i,), memory_space=pltpu.SMEM),
            pl.BlockSpec((tm * TOP_K,), lambda i: (jnp.minimum(i + 1, steps - 1),), memory_space=pltpu.SMEM),
            pl.BlockSpec((tm, D_MODEL), lambda i: (i, 0)),
            pl.BlockSpec((tm, LANES), lambda i: (i, 0)),
            pl.BlockSpec((1, D_MODEL), lambda i: (0, 0)),
            pl.BlockSpec(memory_space=pl.ANY),
        ],
        out_specs=pl.BlockSpec((tm, D_MODEL), lambda i: (i, 0)),
        out_shape=jax.ShapeDtypeStruct((t, D_MODEL), F32),
        scratch_shapes=[pltpu.VMEM((2, TOP_K, tm, ROW_TILES, LANES), F32), pltpu.SemaphoreType.DMA((2,))],
        compiler_params=_cparams(("arbitrary",)),
        name="combine",
    )(dest_flat, dest_flat, h, gate, nw, y_rows)


def _pad_lanes(a, offset, axis):
    n = a.shape[axis]
    pads = [(0, 0)] * a.ndim
    pads[axis] = (offset, LANES - offset - n)
    return jnp.pad(a, pads)


def _layer(x, norm_mix_w, w_in, rg_conv_w, rg_conv_b, rg_gate_a_w, rg_gate_a_b, rg_gate_x_w,
           rg_gate_x_b, rg_lambda, gdn_conv_w, gdn_a_log, gdn_dt_bias, gdn_norm_w, w_branch_rg,
           w_branch_gdn, w_out, norm_ffn_w, router_w, router_b, moe_w_gate_up, moe_b_gate_up,
           moe_w_down, moe_b_down, out_norm_w):
    t = x.shape[0]
    row = lambda a: a.reshape(1, -1).astype(F32)

    small_lo = 2 * RG_WIDTH + 2 * KEY_DIM + 2 * VAL_DIM
    small_hi = small_lo + 2 * V_HEADS
    w_ba = _pad_lanes(w_in[:, small_lo:small_hi], 0, 1)
    w_ba_hi = w_ba.astype(BF16)
    w_ba_lo = (w_ba - w_ba_hi.astype(F32)).astype(BF16)
    z, ba, bat = _inproj(x, row(norm_mix_w), w_in[:, :small_lo].astype(BF16), w_in[:, small_hi:].astype(BF16),
                         jnp.concatenate([w_ba_hi, w_ba_lo], axis=1))

    hg = _rg_branch(z, rg_conv_w, row(rg_conv_b), rg_gate_a_w.astype(BF16), row(rg_gate_a_b),
                    rg_gate_x_w.astype(BF16), row(rg_gate_x_b), row(rg_lambda))

    arow = jnp.stack([_pad_lanes(gdn_a_log, V_HEADS, 0), _pad_lanes(gdn_dt_bias, V_HEADS, 0)])
    o = _gdn_branch(z, gdn_conv_w, ba, bat, arow, arow.T)

    rw = _pad_lanes(router_w, 0, 1)
    rw_hi = rw.astype(BF16)
    rw = jnp.concatenate([rw_hi, (rw - rw_hi.astype(F32)).astype(BF16)], axis=1)
    rb = _pad_lanes(router_b, 0, 0).reshape(1, LANES)
    h, hn, sel, gate, counts = _merge_router(
        x, hg, o, z, w_branch_rg.astype(BF16), w_branch_gdn.astype(BF16), w_out.astype(BF16),
        row(gdn_norm_w), row(norm_ffn_w), rw, rb)

    counts = counts[0, :N_EXPERTS]
    padded = ((counts + MOE_TILE - 1) // MOE_TILE) * MOE_TILE
    pends = jnp.cumsum(padded)
    pstarts = pends - padded
    n_tiles = (t * TOP_K) // MOE_TILE + N_EXPERTS
    tile_start = jnp.arange(n_tiles, dtype=I32) * MOE_TILE
    tile_valid = (tile_start < pends[-1]).astype(I32)
    tile_e = jnp.minimum(jnp.sum(pends[None, :] <= tile_start[:, None], axis=1), N_EXPERTS - 1).astype(I32)
    prev_e = jnp.concatenate([jnp.full((1,), -1, I32), tile_e[:-1]])
    tile_first = jnp.logical_and(tile_e != prev_e, tile_valid > 0).astype(I32)
    tile_slot = ((jnp.cumsum(tile_first) - 1) % 2).astype(I32)
    first_pos = jnp.where(tile_first > 0, jnp.arange(n_tiles, dtype=I32), n_tiles)
    next_first = jnp.concatenate([lax.cummin(first_pos[::-1])[::-1][1:], jnp.full((1,), n_tiles, I32)])
    tile_next = jnp.where(next_first < n_tiles, tile_e[jnp.minimum(next_first, n_tiles - 1)], -1).astype(I32)
    group_end = jnp.any(jnp.logical_and((tile_start + MOE_TILE)[:, None] == pends[None, :],
                                        padded[None, :] > 0), axis=1)
    zflag = jnp.logical_or(group_end, tile_valid == 0).astype(I32)

    dest = _dest_rows(sel, _pad_lanes(pstarts.astype(I32), 0, 0).reshape(1, LANES))
    dest_flat = dest[:, :TOP_K].reshape(t * TOP_K)

    x_rows = _dispatch(dest_flat, zflag, hn)
    y_rows = _experts(tile_e, tile_valid, tile_first, tile_next, tile_slot, x_rows, moe_w_gate_up,
                      moe_b_gate_up.reshape(N_EXPERTS, 1, 2 * D_EXPERT), moe_w_down,
                      moe_b_down.reshape(N_EXPERTS, 1, D_MODEL))

    return _combine(dest_flat, h, gate, row(out_norm_w), y_rows)


def kernel(x, norm_mix_w, w_in, rg_conv_w, rg_conv_b, rg_gate_a_w, rg_gate_a_b, rg_gate_x_w,
           rg_gate_x_b, rg_lambda, gdn_conv_w, gdn_A_log, gdn_dt_bias, gdn_norm_w, w_branch_rg,
           w_branch_gdn, w_out, norm_ffn_w, router_w, router_b, moe_w_gate_up, moe_b_gate_up,
           moe_w_down, moe_b_down, norm_final_w):
    assert x.shape[0] == 1 and norm_mix_w.shape[0] == 1
    out = _layer(x[0], norm_mix_w[0], w_in[0], rg_conv_w[0], rg_conv_b[0], rg_gate_a_w[0],
                 rg_gate_a_b[0], rg_gate_x_w[0], rg_gate_x_b[0], rg_lambda[0], gdn_conv_w[0],
                 gdn_A_log[0], gdn_dt_bias[0], gdn_norm_w[0], w_branch_rg[0], w_branch_gdn[0],
                 w_out[0], norm_ffn_w[0], router_w[0], router_b[0], moe_w_gate_up[0],
                 moe_b_gate_up[0], moe_w_down[0], moe_b_down[0], norm_final_w)
    return out[None]
```

```python
import functools

import jax
import jax.numpy as jnp
from jax import lax
from jax.experimental import pallas as pl
from jax.experimental.pallas import tpu as pltpu

F32 = jnp.float32
BF16 = jnp.bfloat16
I32 = jnp.int32
HI = lax.Precision.HIGHEST

D_MODEL = 1024
RG_WIDTH = 1024
RG_BLOCKS = 4
RG_BLOCK_DIM = RG_WIDTH // RG_BLOCKS
RG_C = 8.0
CONV_K = 4
QK_HEADS = 8
V_HEADS = 16
HEAD_DIM = 128
KEY_DIM = QK_HEADS * HEAD_DIM
VAL_DIM = V_HEADS * HEAD_DIM
CHUNK = 64
N_EXPERTS = 32
TOP_K = 4
D_EXPERT = 1024
SWIGLU_ALPHA = 1.702
SWIGLU_LIMIT = 7.0
EPS = 1e-6

LANES = 128
SUBLANES = 8
VMEM_LIMIT = 56 * 1024 * 1024

Z_RGX, Z_RGY, Z_Q, Z_K = 0, 1, 2, 3
Z_V, Z_ZG = 2, 3
Z_GA, Z_GB = 8, 9
Z_WIDTH = 10 * 1024

TM_INPROJ = 1024
TN_INPROJ = 2048
TM_RG = 256
GDN_NCH = 2
TM_MERGE = 512
TM_ROWS = 512
MOE_TILE = 512


def _cparams(sem):
    return pltpu.CompilerParams(dimension_semantics=sem, vmem_limit_bytes=VMEM_LIMIT)


def _softplus(x):
    return jnp.maximum(x, 0.0) + jnp.log1p(jnp.exp(-jnp.abs(x)))


def _sigmoid(x):
    return 1.0 / (1.0 + jnp.exp(-x))


def _silu(x):
    return x * _sigmoid(x)


def _gelu_tanh(x):
    c = 0.7978845608028654
    return 0.5 * x * (1.0 + jnp.tanh(c * (x + 0.044715 * (x * x * x))))


def _rms(x, w):
    return x * lax.rsqrt(jnp.mean(x * x, axis=-1, keepdims=True) + EPS) * w


def _dot(a, b):
    return jnp.dot(a, b, preferred_element_type=F32)


ROW_TILES = D_MODEL // LANES


def _rows_to_tiles(x):
    parts = jnp.stack([x[:, s * LANES:(s + 1) * LANES] for s in range(ROW_TILES)], axis=0)
    return pltpu.einshape("snl->nsl", parts)


def _tiles_to_rows(x):
    parts = pltpu.einshape("nsl->snl", x)
    return jnp.concatenate([parts[s] for s in range(ROW_TILES)], axis=1)


def _dot_hi(a, b):
    return jnp.dot(a, b, precision=HI, preferred_element_type=F32)


def _bf16_terms(x):
    t0 = x.astype(BF16)
    r = x - t0.astype(F32)
    t1 = r.astype(BF16)
    t2 = (r - t1.astype(F32)).astype(BF16)
    return t0, t1, t2


def _dot_nt(a, b, precision=None):
    return lax.dot_general(a, b, (((1,), (1,)), ((), ())), precision=precision,
                           preferred_element_type=F32)


def _dot_tn(a, b):
    return lax.dot_general(a, b, (((0,), (0,)), ((), ())), preferred_element_type=F32)


def _inproj_kernel(x_ref, nw_ref, w_ref, wba_ref, z_ref, ba_ref, bat_ref, xn_sc):
    @pl.when(pl.program_id(1) == 0)
    def _():
        xn = _rms(x_ref[...], nw_ref[...])
        x_hi = xn.astype(BF16)
        xn_sc[...] = x_hi
        x_lo = (xn - x_hi.astype(F32)).astype(BF16)
        wba = wba_ref[...]
        hi = _dot(x_hi, wba)
        ba = hi[:, :LANES] + hi[:, LANES:] + _dot(x_lo, wba[:, :LANES])
        ba_ref[...] = ba
        bat_ref[...] = ba.T

    z_ref[...] = _dot(xn_sc[...], w_ref[...]).astype(BF16)


def _inproj(x, nw, w_z, w_ba):
    t = x.shape[1]
    tm, tn = min(TM_INPROJ, t), TN_INPROJ
    return pl.pallas_call(
        _inproj_kernel,
        grid=(t // tm, Z_WIDTH // tn),
        in_specs=[
            pl.BlockSpec((None, tm, D_MODEL), lambda i, j: (0, i, 0)),
            pl.BlockSpec((1, D_MODEL), lambda i, j: (0, 0)),
            pl.BlockSpec((D_MODEL, tn), lambda i, j: (0, j)),
            pl.BlockSpec((D_MODEL, 2 * LANES), lambda i, j: (0, 0)),
        ],
        out_specs=[
            pl.BlockSpec((tm, tn), lambda i, j: (i, j)),
            pl.BlockSpec((tm, LANES), lambda i, j: (i, 0)),
            pl.BlockSpec((LANES, tm), lambda i, j: (0, i)),
        ],
        out_shape=[
            jax.ShapeDtypeStruct((t, Z_WIDTH), BF16),
            jax.ShapeDtypeStruct((t, LANES), F32),
            jax.ShapeDtypeStruct((LANES, t), F32),
        ],
        scratch_shapes=[pltpu.VMEM((tm, D_MODEL), BF16)],
        compiler_params=_cparams(("arbitrary", "arbitrary")),
        name="inproj",
    )(x, nw, w_z, w_ba)


def _delay_matrix(tm):
    t = jnp.arange(tm)
    return jnp.concatenate([(t[None, :] == t[:, None] - s) for s in range(1, CONV_K)], axis=0).astype(BF16)


def _causal_conv(x_ref, hist_sc, cw_ref, delay_ref, lo):
    tm, width = x_ref.shape
    cols = slice(lo, lo + width)
    xb = x_ref[...]
    x = xb.astype(F32)
    delayed = _dot(delay_ref[...], xb)
    acc = cw_ref[CONV_K - 1:CONV_K, cols] * x
    prev = hist_sc[:, cols]
    row = lax.broadcasted_iota(I32, (SUBLANES, 1), 0)
    head = jnp.zeros((SUBLANES, width), F32)
    for s in range(1, CONV_K):
        tap = cw_ref[CONV_K - 1 - s:CONV_K - s, cols]
        acc = acc + tap * delayed[(s - 1) * tm:s * tm, :]
        head = head + tap * jnp.where(row < s, pltpu.roll(prev, s, axis=0), 0.0)
    hist_sc[:, cols] = x[tm - SUBLANES:, :]
    return jnp.concatenate([acc[:SUBLANES, :] + head, acc[SUBLANES:, :]], axis=0)


def _rg_kernel(x_ref, y_ref, cw_ref, cb_ref, wa_ref, ba_ref, wx_ref, bx_ref, lam_ref, delay_ref,
               out_ref, prev_sc, h_sc):
    i = pl.program_id(0)
    tm = x_ref.shape[0]

    @pl.when(i == 0)
    def _():
        prev_sc[...] = jnp.zeros_like(prev_sc)
        h_sc[...] = jnp.zeros_like(h_sc)

    xa = _causal_conv(x_ref, prev_sc, cw_ref, delay_ref, 0) + cb_ref[...]

    xab = xa.astype(BF16)
    r_parts, i_parts = [], []
    for blk in range(RG_BLOCKS):
        xs = xab[:, blk * RG_BLOCK_DIM:(blk + 1) * RG_BLOCK_DIM]
        r_parts.append(_dot(xs, wa_ref[blk]))
        i_parts.append(_dot(xs, wx_ref[blk]))
    r = _sigmoid(jnp.concatenate(r_parts, axis=1) + ba_ref[...])
    ig = _sigmoid(jnp.concatenate(i_parts, axis=1) + bx_ref[...])

    log_a = (-RG_C) * r * _softplus(-lam_ref[...])
    a = jnp.exp(log_a)
    m2 = jnp.maximum(1.0 - a * a, 0.0)
    mult = jnp.where(m2 > 0.0, m2 * lax.rsqrt(m2), 0.0)
    rows = lax.broadcasted_iota(I32, (tm, 1), 0)
    mult = jnp.where(jnp.logical_and(rows == 0, i == 0), 1.0, mult)
    b = mult * (ig * xa)

    groups = tm // SUBLANES
    a = a.reshape(groups, SUBLANES, RG_WIDTH)
    b = b.reshape(groups, SUBLANES, RG_WIDTH)
    sub = lax.broadcasted_iota(I32, (1, SUBLANES, 1), 1)
    s = 1
    while s < SUBLANES:
        keep = sub >= s
        a_sh = pltpu.roll(a, s, axis=1)
        b_sh = pltpu.roll(b, s, axis=1)
        b = b + jnp.where(keep, a * b_sh, 0.0)
        a = jnp.where(keep, a * a_sh, a)
        s *= 2
    h_prev = h_sc[...]
    hs = []
    for g in range(groups):
        hs.append(b[g] + a[g] * h_prev)
        h_prev = hs[-1][SUBLANES - 1:SUBLANES, :]
    h_sc[...] = h_prev
    h = jnp.concatenate(hs, axis=0)
    out_ref[...] = (h * _gelu_tanh(y_ref[...].astype(F32))).astype(BF16)


def _rg_branch(z, cw, cb, wa, ba, wx, bx, lam):
    t = z.shape[0]
    tm = min(TM_RG, t)
    full = lambda shape: pl.BlockSpec(shape, lambda i: (0,) * len(shape))
    return pl.pallas_call(
        _rg_kernel,
        grid=(t // tm,),
        in_specs=[
            pl.BlockSpec((tm, RG_WIDTH), lambda i: (i, Z_RGX)),
            pl.BlockSpec((tm, RG_WIDTH), lambda i: (i, Z_RGY)),
            full((CONV_K, RG_WIDTH)),
            full((1, RG_WIDTH)),
            full((RG_BLOCKS, RG_BLOCK_DIM, RG_BLOCK_DIM)),
            full((1, RG_WIDTH)),
            full((RG_BLOCKS, RG_BLOCK_DIM, RG_BLOCK_DIM)),
            full((1, RG_WIDTH)),
            full((1, RG_WIDTH)),
            full(((CONV_K - 1) * tm, tm)),
        ],
        out_specs=pl.BlockSpec((tm, RG_WIDTH), lambda i: (i, 0)),
        out_shape=jax.ShapeDtypeStruct((t, RG_WIDTH), BF16),
        scratch_shapes=[pltpu.VMEM((SUBLANES, RG_WIDTH), F32), pltpu.VMEM((1, RG_WIDTH), F32)],
        compiler_params=_cparams(("arbitrary",)),
        name="rg_branch",
    )(z, z, cw, cb, wa, ba, wx, bx, lam, _delay_matrix(tm))


def _l2norm_heads(x, n_heads):
    parts = []
    for h in range(n_heads):
        xh = x[:, h * HEAD_DIM:(h + 1) * HEAD_DIM]
        parts.append(xh * lax.rsqrt(jnp.sum(xh * xh, axis=-1, keepdims=True) + EPS))
    return jnp.concatenate(parts, axis=1)


def _conv_silu(x_ref, prev_sc, cw_ref, delay_ref, lo):
    return _silu(_causal_conv(x_ref, prev_sc, cw_ref, delay_ref, lo))


def _unit_lower_inverses(mats):
    c = mats[0].shape[0]
    ri = lax.broadcasted_iota(I32, (c, c), 0)
    ci = lax.broadcasted_iota(I32, (c, c), 1)
    eye = jnp.where(ri == ci, 1.0, 0.0)
    ps = [eye - a for a in mats]
    xbs = [a.astype(BF16) for a in mats]
    xbs = [_dot(xb, xb).astype(BF16) for xb in xbs]
    power = 2
    while power < c:
        ps = [p + _dot(p.astype(BF16), xb) for p, xb in zip(ps, xbs)]
        power *= 2
        if power < c:
            xbs = [_dot(xb, xb).astype(BF16) for xb in xbs]
    return ps


def _gdn_kernel(q_ref, k_ref, v_ref, cw_ref, ba_ref, bat_ref, arow_ref, acol_ref, msum_ref, delay_ref,
                o_ref, s_sc, gt_sc, prev_sc, *, nch):
    hg_n = V_HEADS
    c = CHUNK
    tile = nch * c
    scale = HEAD_DIM ** -0.5

    @pl.when(pl.program_id(0) == 0)
    def _():
        s_sc[...] = jnp.zeros_like(s_sc)
        prev_sc[...] = jnp.zeros_like(prev_sc)

    q_all = _l2norm_heads(_conv_silu(q_ref, prev_sc, cw_ref, delay_ref, 0), QK_HEADS)
    k_all = _l2norm_heads(_conv_silu(k_ref, prev_sc, cw_ref, delay_ref, KEY_DIM), QK_HEADS)
    v_all = _conv_silu(v_ref, prev_sc, cw_ref, delay_ref, 2 * KEY_DIM)
    q_bf, k_bf = q_all.astype(BF16), k_all.astype(BF16)

    ba = ba_ref[...]
    arow = arow_ref[...]
    g_all = -jnp.exp(arow[0:1, :]) * _softplus(ba + arow[1:2, :])
    beta_all = _sigmoid(ba)
    msum = msum_ref[...]
    sums = sum(_dot(msum, term) for term in _bf16_terms(g_all))
    gcs = sums[:tile, :]
    gls = sums[tile:, :]
    egc = jnp.exp(gcs)
    ekd = jnp.exp(gls - gcs)
    acol = acol_ref[...]
    g_t = -jnp.exp(acol[:, 0:1]) * _softplus(bat_ref[...] + acol[:, 1:2])
    gt_sc[...] = sum(_dot_nt(term, msum[:tile, :]) for term in _bf16_terms(g_t))

    lane = lax.broadcasted_iota(I32, (tile, LANES), 1)
    ri = lax.broadcasted_iota(I32, (c, c), 0)
    ci = lax.broadcasted_iota(I32, (c, c), 1)
    causal = ri >= ci
    strict = ri > ci

    def column(arr, lane_idx):
        return jnp.sum(jnp.where(lane == lane_idx, arr, 0.0), axis=1, keepdims=True)

    cols = []
    for j in range(hg_n):
        head = j
        b_col = column(beta_all, head)
        egc_col = column(egc, V_HEADS + head)
        cols.append(dict(
            b=b_col, gcs=column(gcs, V_HEADS + head), egc=egc_col,
            ekd=column(ekd, V_HEADS + head), be=b_col * egc_col,
            g_row=gt_sc[pl.ds(V_HEADS + head, 1), :]))

    pairs = [(n, j) for n in range(nch) for j in range(hg_n)]
    rows_of = lambda n: slice(n * c, (n + 1) * c)
    head_cols = lambda j: slice(j * HEAD_DIM, (j + 1) * HEAD_DIM)
    kq = {}
    for n in range(nch):
        for qh in range(hg_n // 2):
            k = k_bf[rows_of(n), head_cols(qh)]
            q = q_bf[rows_of(n), head_cols(qh)]
            kq[(n, qh)] = dict(kk=_dot_nt(k, k), qk=_dot_nt(q, k), kf=k_all[rows_of(n), head_cols(qh)],
                               qf=q_all[rows_of(n), head_cols(qh)])
    decays, a_mats = {}, []
    for n, j in pairs:
        col = cols[j]
        diff = col["gcs"][rows_of(n), :] - col["g_row"][:, rows_of(n)]
        decay = jnp.where(causal, jnp.exp(jnp.where(causal, diff, 0.0)), 0.0)
        decays[(n, j)] = decay
        a_mats.append(jnp.where(strict, col["b"][rows_of(n), :] * kq[(n, j // 2)]["kk"] * decay, 0.0))
    t_mats = _unit_lower_inverses(a_mats)
    uws, attns, q_decs, k_decs = {}, {}, {}, {}
    for (n, j), t_mat in zip(pairs, t_mats):
        col, kqn = cols[j], kq[(n, j // 2)]
        v = v_all[rows_of(n), head_cols(j)]
        rhs = jnp.concatenate([v * col["b"][rows_of(n), :], kqn["kf"] * col["be"][rows_of(n), :]], axis=1)
        uws[(n, j)] = _dot(t_mat.astype(BF16), rhs.astype(BF16))
        attns[(n, j)] = (kqn["qk"] * decays[(n, j)]).astype(BF16)
        q_decs[(n, j)] = kqn["qf"] * col["egc"][rows_of(n), :]
        k_decs[(n, j)] = (kqn["kf"] * col["ekd"][rows_of(n), :]).astype(BF16)

    states = [s_sc[j] for j in range(hg_n)]
    for n in range(nch):
        heads = range(hg_n)
        ws_qs = [_dot(jnp.concatenate([uws[(n, j)][:, HEAD_DIM:], q_decs[(n, j)]], axis=0).astype(BF16),
                      states[j].astype(BF16)) for j in heads]
        v_news = [(uws[(n, j)][:, :HEAD_DIM] - ws_qs[j][:c, :]).astype(BF16) for j in heads]
        outs = [(ws_qs[j][c:, :] + _dot(attns[(n, j)], v_news[j])) * scale for j in heads]
        for j in heads:
            o_ref[rows_of(n), head_cols(j)] = outs[j].astype(BF16)
        egls = [jnp.exp(cols[j]["g_row"][:, (n + 1) * c - 1:(n + 1) * c]) for j in heads]
        states = [states[j] * egls[j] + _dot_tn(k_decs[(n, j)], v_news[j]) for j in heads]
    for j in range(hg_n):
        s_sc[j] = states[j]


def _gdn_branch(z, cw, ba, bat, arow, acol):
    t = z.shape[0]
    nch = GDN_NCH
    tile = nch * CHUNK
    idx = jnp.arange(tile)
    same = (idx[:, None] // CHUNK) == (idx[None, :] // CHUNK)
    mtri = jnp.logical_and(same, idx[:, None] >= idx[None, :])
    msum = jnp.concatenate([mtri, same], axis=0).astype(BF16)
    return pl.pallas_call(
        functools.partial(_gdn_kernel, nch=nch),
        grid=(t // tile,),
        in_specs=[
            pl.BlockSpec((tile, KEY_DIM), lambda n: (n, Z_Q)),
            pl.BlockSpec((tile, KEY_DIM), lambda n: (n, Z_K)),
            pl.BlockSpec((tile, VAL_DIM), lambda n: (n, Z_V)),
            pl.BlockSpec((CONV_K, 2 * KEY_DIM + VAL_DIM), lambda n: (0, 0)),
            pl.BlockSpec((tile, LANES), lambda n: (n, 0)),
            pl.BlockSpec((LANES, tile), lambda n: (0, n)),
            pl.BlockSpec((2, LANES), lambda n: (0, 0)),
            pl.BlockSpec((LANES, 2), lambda n: (0, 0)),
            pl.BlockSpec((2 * tile, tile), lambda n: (0, 0)),
            pl.BlockSpec(((CONV_K - 1) * tile, tile), lambda n: (0, 0)),
        ],
        out_specs=pl.BlockSpec((tile, VAL_DIM), lambda n: (n, 0)),
        out_shape=jax.ShapeDtypeStruct((t, VAL_DIM), BF16),
        scratch_shapes=[pltpu.VMEM((V_HEADS, HEAD_DIM, HEAD_DIM), F32), pltpu.VMEM((LANES, tile), F32),
                        pltpu.VMEM((SUBLANES, 2 * KEY_DIM + VAL_DIM), F32)],
        compiler_params=_cparams(("arbitrary",)),
        name="gdn_branch",
    )(z, z, z, cw, ba, bat, arow, acol, msum, _delay_matrix(tile))


def _merge_kernel(x_ref, hg_ref, o_ref, zg_ref, ga_ref, gb_ref, wrg_ref, wgdn_ref, wout_ref,
                  gnw_ref, fnw_ref, rw_ref, rb_ref, lstrict_ref,
                  h_ref, hn_ref, sel_ref, gate_ref, cnt_ref, carry_sc):
    tm = x_ref.shape[0]

    @pl.when(pl.program_id(0) == 0)
    def _():
        carry_sc[...] = jnp.zeros_like(carry_sc)

    ya = _dot(hg_ref[...], wrg_ref[...])
    o = o_ref[...].astype(F32)
    zg = zg_ref[...].astype(F32)
    gnw = gnw_ref[...]
    parts = []
    for h in range(V_HEADS):
        sl = slice(h * HEAD_DIM, (h + 1) * HEAD_DIM)
        parts.append((_rms(o[:, sl], gnw) * _silu(zg[:, sl])).astype(BF16))
    yb = _dot(jnp.concatenate(parts, axis=1), wgdn_ref[...])
    mix = _sigmoid(ga_ref[...].astype(F32)) * ya + _sigmoid(gb_ref[...].astype(F32)) * yb
    h = x_ref[...] + _dot(mix.astype(BF16), wout_ref[...])
    h_ref[...] = h

    hn = _rms(h, fnw_ref[...])
    hn_ref[...] = _rows_to_tiles(hn)
    lane = lax.broadcasted_iota(I32, (tm, LANES), 1)
    lane_f = lane.astype(F32)
    neg_inf = float("-inf")
    hn_hi = hn.astype(BF16)
    hn_lo = (hn - hn_hi.astype(F32)).astype(BF16)
    rw = rw_ref[...]
    hi = _dot(hn_hi, rw)
    raw = hi[:, :LANES] + hi[:, LANES:] + _dot(hn_lo, rw[:, :LANES])
    logits = jnp.where(lane < N_EXPERTS, raw + rb_ref[...], neg_inf)
    vals, idxs = [], []
    multi_hot = jnp.zeros((tm, LANES), F32)
    for _ in range(TOP_K):
        m = jnp.max(logits, axis=-1, keepdims=True)
        first = jnp.min(jnp.where(logits == m, lane_f, float(LANES)), axis=-1, keepdims=True)
        hit = lane_f == first
        multi_hot = multi_hot + jnp.where(hit, 1.0, 0.0)
        logits = jnp.where(hit, neg_inf, logits)
        vals.append(m)
        idxs.append(first)
    exps = [jnp.exp(v - vals[0]) for v in vals]
    denom = exps[0] + exps[1] + exps[2] + exps[3]

    before = _dot(lstrict_ref[...], multi_hot.astype(BF16)) + carry_sc[...]
    carry_sc[...] = carry_sc[...] + jnp.sum(multi_hot, axis=0, keepdims=True)
    cnt_ref[...] = carry_sc[...].astype(I32)

    sel = jnp.zeros((tm, LANES), F32)
    gate = jnp.zeros((tm, LANES), F32)
    for kk in range(TOP_K):
        rank = jnp.sum(jnp.where(lane_f == idxs[kk], before, 0.0), axis=-1, keepdims=True)
        sel = jnp.where(lane == kk, idxs[kk], sel)
        sel = jnp.where(lane == TOP_K + kk, rank, sel)
        gate = jnp.where(lane == kk, exps[kk] / denom, gate)
    sel_ref[...] = sel.astype(I32)
    gate_ref[...] = gate


def _merge_router(x, hg, o, z, wrg, wgdn, wout, gnw, fnw, rw, rb):
    t = x.shape[1]
    tm = min(TM_MERGE, t)
    idx = jnp.arange(tm)
    lstrict = (idx[:, None] > idx[None, :]).astype(BF16)
    full = lambda shape: pl.BlockSpec(shape, lambda i: (0,) * len(shape))
    row = lambda width, col=0: pl.BlockSpec((tm, width), lambda i: (i, col))
    return pl.pallas_call(
        _merge_kernel,
        grid=(t // tm,),
        in_specs=[
            pl.BlockSpec((None, tm, D_MODEL), lambda i: (0, i, 0)),
            row(RG_WIDTH), row(VAL_DIM), row(VAL_DIM, Z_ZG), row(D_MODEL, Z_GA),
            row(D_MODEL, Z_GB),
            full((RG_WIDTH, D_MODEL)), full((VAL_DIM, D_MODEL)), full((D_MODEL, D_MODEL)),
            full((1, HEAD_DIM)), full((1, D_MODEL)), full((D_MODEL, 2 * LANES)), full((1, LANES)),
            full((tm, tm)),
        ],
        out_specs=[row(D_MODEL), pl.BlockSpec((tm, ROW_TILES, LANES), lambda i: (i, 0, 0)), row(LANES),
                   row(LANES), full((1, LANES))],
        out_shape=[
            jax.ShapeDtypeStruct((t, D_MODEL), F32),
            jax.ShapeDtypeStruct((t, ROW_TILES, LANES), F32),
            jax.ShapeDtypeStruct((t, LANES), I32),
            jax.ShapeDtypeStruct((t, LANES), F32),
            jax.ShapeDtypeStruct((1, LANES), I32),
        ],
        scratch_shapes=[pltpu.VMEM((1, LANES), F32)],
        compiler_params=_cparams(("arbitrary",)),
        name="merge_router",
    )(x, hg, o, z, z, z, wrg, wgdn, wout, gnw, fnw, rw, rb, lstrict)


def _dest_kernel(sel_ref, pstart_ref, dest_ref):
    tm = sel_ref.shape[0]
    sel = sel_ref[...].astype(F32)
    lane = lax.broadcasted_iota(I32, (tm, LANES), 1)
    lane_f = lane.astype(F32)
    pstart = pstart_ref[...].astype(F32)
    dest = jnp.zeros((tm, LANES), F32)
    for kk in range(TOP_K):
        e = jnp.sum(jnp.where(lane == kk, sel, 0.0), axis=-1, keepdims=True)
        rank = jnp.sum(jnp.where(lane == TOP_K + kk, sel, 0.0), axis=-1, keepdims=True)
        base = jnp.sum(jnp.where(lane_f == e, pstart, 0.0), axis=-1, keepdims=True)
        dest = jnp.where(lane == kk, base + rank, dest)
    dest_ref[...] = dest.astype(I32)


def _dest_rows(sel, pstart):
    t = sel.shape[0]
    tm = min(1024, t)
    return pl.pallas_call(
        _dest_kernel,
        grid=(t // tm,),
        in_specs=[pl.BlockSpec((tm, LANES), lambda i: (i, 0)), pl.BlockSpec((1, LANES), lambda i: (0, 0))],
        out_specs=pl.BlockSpec((tm, LANES), lambda i: (i, 0)),
        out_shape=jax.ShapeDtypeStruct((t, LANES), I32),
        compiler_params=_cparams(("arbitrary",)),
        name="dest_rows",
    )(sel, pstart)


def _row_copy(src_ref, src_row, dst_ref, dst_row, sem):
    return pltpu.make_async_copy(src_ref.at[pl.ds(src_row, 1)], dst_ref.at[pl.ds(dst_row, 1)], sem)


def _dispatch_kernel(dest_ref, zflag_ref, hn_ref, xr_ref, zero_sc, sem, zsem):
    tm = hn_ref.shape[0]
    n_tiles = zflag_ref.shape[0]

    def zero_copy(tile):
        start = pl.multiple_of(tile * MOE_TILE, MOE_TILE)
        return pltpu.make_async_copy(zero_sc, xr_ref.at[pl.ds(start, MOE_TILE)], zsem)

    @pl.when(pl.program_id(0) == 0)
    def _():
        zero_sc[...] = jnp.zeros_like(zero_sc)

        def start(tile, carry):
            @pl.when(zflag_ref[tile] > 0)
            def _():
                zero_copy(tile).start()
            return carry

        def wait(tile, carry):
            @pl.when(zflag_ref[tile] > 0)
            def _():
                zero_copy(tile).wait()
            return carry

        lax.fori_loop(0, n_tiles, start, 0)
        lax.fori_loop(0, n_tiles, wait, 0)

    def issue(tok, carry):
        for kk in range(TOP_K):
            _row_copy(hn_ref, tok, xr_ref, dest_ref[tok * TOP_K + kk], sem).start()
        return carry

    lax.fori_loop(0, tm, issue, 0)

    def drain(tok, carry):
        for kk in range(TOP_K):
            _row_copy(hn_ref, tok, xr_ref, dest_ref[tok * TOP_K + kk], sem).wait()
        return carry

    lax.fori_loop(0, tm, drain, 0)


def _dispatch(dest_flat, zflag, hn):
    t = hn.shape[0]
    tm = min(TM_ROWS, t)
    n_tiles = zflag.shape[0]
    return pl.pallas_call(
        _dispatch_kernel,
        grid=(t // tm,),
        in_specs=[
            pl.BlockSpec((tm * TOP_K,), lambda i: (i,), memory_space=pltpu.SMEM),
            pl.BlockSpec((n_tiles,), lambda i: (0,), memory_space=pltpu.SMEM),
            pl.BlockSpec((tm, ROW_TILES, LANES), lambda i: (i, 0, 0)),
        ],
        out_specs=pl.BlockSpec(memory_space=pl.ANY),
        out_shape=jax.ShapeDtypeStruct((n_tiles * MOE_TILE, ROW_TILES, LANES), F32),
        scratch_shapes=[pltpu.VMEM((MOE_TILE, ROW_TILES, LANES), F32), pltpu.SemaphoreType.DMA,
                        pltpu.SemaphoreType.DMA],
        compiler_params=_cparams(("arbitrary",)),
        name="dispatch",
    )(dest_flat, zflag, hn)


def _expert_kernel(te_ref, tv_ref, tf_ref, tn_ref, ts_ref, x_ref, bgu_ref, bd_ref, wgu_hbm, wd_hbm, y_ref,
                   wgu_f32, wd_f32, wgu_sc, wd_sc, sems):
    i = pl.program_id(0)

    def weight_copies(expert, slot):
        return (pltpu.make_async_copy(wgu_hbm.at[expert], wgu_f32.at[slot], sems.at[0, slot]),
                pltpu.make_async_copy(wd_hbm.at[expert], wd_f32.at[slot], sems.at[1, slot]))

    @pl.when(i == 0)
    def _():
        for copy in weight_copies(te_ref[0], ts_ref[0]):
            copy.start()

    @pl.when(tf_ref[i] > 0)
    def _():
        slot = ts_ref[i]
        for copy in weight_copies(te_ref[i], slot):
            copy.wait()

        @pl.when(tn_ref[i] >= 0)
        def _():
            for copy in weight_copies(tn_ref[i], 1 - slot):
                copy.start()

        wgu_sc[...] = wgu_f32[slot].astype(BF16)
        wd_sc[...] = wd_f32[slot].astype(BF16)

    @pl.when(tv_ref[i] > 0)
    def _():
        gu = _dot(_tiles_to_rows(x_ref[...]).astype(BF16), wgu_sc[...]) + bgu_ref[0]
        gate = jnp.minimum(gu[:, :D_EXPERT], SWIGLU_LIMIT)
        up = jnp.clip(gu[:, D_EXPERT:], -SWIGLU_LIMIT, SWIGLU_LIMIT)
        hid = (up + 1.0) * (gate * _sigmoid(SWIGLU_ALPHA * gate))
        y_ref[...] = _rows_to_tiles(_dot(hid.astype(BF16), wd_sc[...]) + bd_ref[0])

    @pl.when(tv_ref[i] == 0)
    def _():
        y_ref[...] = jnp.zeros_like(y_ref)


def _experts(tile_e, tile_valid, tile_first, tile_next, tile_slot, x_rows, wgu, bgu, wd, bd):
    n_rows = x_rows.shape[0]
    n_tiles = n_rows // MOE_TILE
    by_expert = lambda i, te, tv, tf, tn, ts: (te[i], 0, 0)
    grid_spec = pltpu.PrefetchScalarGridSpec(
        num_scalar_prefetch=5,
        grid=(n_tiles,),
        in_specs=[
            pl.BlockSpec((MOE_TILE, ROW_TILES, LANES), lambda i, te, tv, tf, tn, ts: (i, 0, 0)),
            pl.BlockSpec((1, 1, 2 * D_EXPERT), by_expert),
            pl.BlockSpec((1, 1, D_MODEL), by_expert),
            pl.BlockSpec(memory_space=pl.ANY),
            pl.BlockSpec(memory_space=pl.ANY),
        ],
        out_specs=pl.BlockSpec((MOE_TILE, ROW_TILES, LANES), lambda i, te, tv, tf, tn, ts: (i, 0, 0)),
        scratch_shapes=[
            pltpu.VMEM((2, D_MODEL, 2 * D_EXPERT), F32), pltpu.VMEM((2, D_EXPERT, D_MODEL), F32),
            pltpu.VMEM((D_MODEL, 2 * D_EXPERT), BF16), pltpu.VMEM((D_EXPERT, D_MODEL), BF16),
            pltpu.SemaphoreType.DMA((2, 2)),
        ],
    )
    return pl.pallas_call(
        _expert_kernel,
        grid_spec=grid_spec,
        out_shape=jax.ShapeDtypeStruct((n_rows, ROW_TILES, LANES), F32),
        compiler_params=_cparams(("arbitrary",)),
        name="experts",
    )(tile_e, tile_valid, tile_first, tile_next, tile_slot, x_rows, bgu, bd, wgu, wd)


def _combine_kernel(dest_ref, h_ref, gate_ref, nw_ref, yr_ref, out_ref, buf_sc, sem):
    tm = h_ref.shape[0]

    def row_copy(tok, kk):
        return _row_copy(yr_ref, dest_ref[tok * TOP_K + kk], buf_sc.at[kk], tok, sem)

    def issue(tok, carry):
        for kk in range(TOP_K):
            row_copy(tok, kk).start()
        return carry

    lax.fori_loop(0, tm, issue, 0)

    def drain(tok, carry):
        for kk in range(TOP_K):
            row_copy(tok, kk).wait()
        return carry

    lax.fori_loop(0, tm, drain, 0)

    gate = gate_ref[...]
    acc = h_ref[...]
    for kk in range(TOP_K):
        acc = acc + gate[:, kk:kk + 1] * _tiles_to_rows(buf_sc[kk])
    out_ref[...] = _rms(acc, nw_ref[...])


def _combine(dest_flat, h, gate, nw, y_rows):
    t = h.shape[0]
    tm = min(TM_ROWS, t)
    return pl.pallas_call(
        _combine_kernel,
        grid=(t // tm,),
        in_specs=[
            pl.BlockSpec((tm * TOP_K,), lambda i: (i,), memory_space=pltpu.SMEM),
            pl.BlockSpec((tm, D_MODEL), lambda i: (i, 0)),
            pl.BlockSpec((tm, LANES), lambda i: (i, 0)),
            pl.BlockSpec((1, D_MODEL), lambda i: (0, 0)),
            pl.BlockSpec(memory_space=pl.ANY),
        ],
        out_specs=pl.BlockSpec((None, tm, D_MODEL), lambda i: (0, i, 0)),
        out_shape=jax.ShapeDtypeStruct((1, t, D_MODEL), F32),
        scratch_shapes=[pltpu.VMEM((TOP_K, tm, ROW_TILES, LANES), F32), pltpu.SemaphoreType.DMA],
        compiler_params=_cparams(("arbitrary",)),
        name="combine",
    )(dest_flat, h, gate, nw, y_rows)


def _pad_lanes(a, offset, axis):
    n = a.shape[axis]
    pads = [(0, 0)] * a.ndim
    pads[axis] = (offset, LANES - offset - n)
    return jnp.pad(a, pads)


def _layer(x, norm_mix_w, w_in, rg_conv_w, rg_conv_b, rg_gate_a_w, rg_gate_a_b, rg_gate_x_w,
           rg_gate_x_b, rg_lambda, gdn_conv_w, gdn_a_log, gdn_dt_bias, gdn_norm_w, w_branch_rg,
           w_branch_gdn, w_out, norm_ffn_w, router_w, router_b, moe_w_gate_up, moe_b_gate_up,
           moe_w_down, moe_b_down, out_norm_w):
    t = x.shape[1]
    row = lambda a: a.reshape(1, -1).astype(F32)

    small_lo = 2 * RG_WIDTH + 2 * KEY_DIM + 2 * VAL_DIM
    small_hi = small_lo + 2 * V_HEADS
    w_z = jnp.concatenate([w_in[:, :small_lo].astype(BF16), w_in[:, small_hi:].astype(BF16)], axis=1)
    w_ba = _pad_lanes(w_in[:, small_lo:small_hi], 0, 1)
    w_ba_hi = w_ba.astype(BF16)
    w_ba_lo = (w_ba - w_ba_hi.astype(F32)).astype(BF16)
    z, ba, bat = _inproj(x, row(norm_mix_w), w_z, jnp.concatenate([w_ba_hi, w_ba_lo], axis=1))

    hg = _rg_branch(z, rg_conv_w, row(rg_conv_b), rg_gate_a_w.astype(BF16), row(rg_gate_a_b),
                    rg_gate_x_w.astype(BF16), row(rg_gate_x_b), row(rg_lambda))

    arow = jnp.stack([_pad_lanes(gdn_a_log, V_HEADS, 0), _pad_lanes(gdn_dt_bias, V_HEADS, 0)])
    o = _gdn_branch(z, gdn_conv_w, ba, bat, arow, arow.T)

    rw = _pad_lanes(router_w, 0, 1)
    rw_hi = rw.astype(BF16)
    rw = jnp.concatenate([rw_hi, (rw - rw_hi.astype(F32)).astype(BF16)], axis=1)
    rb = _pad_lanes(router_b, 0, 0).reshape(1, LANES)
    h, hn, sel, gate, counts = _merge_router(
        x, hg, o, z, w_branch_rg.astype(BF16), w_branch_gdn.astype(BF16), w_out.astype(BF16),
        row(gdn_norm_w), row(norm_ffn_w), rw, rb)

    counts = counts[0, :N_EXPERTS]
    padded = ((counts + MOE_TILE - 1) // MOE_TILE) * MOE_TILE
    pends = jnp.cumsum(padded)
    pstarts = pends - padded
    n_tiles = (t * TOP_K) // MOE_TILE + N_EXPERTS
    tile_start = jnp.arange(n_tiles, dtype=I32) * MOE_TILE
    tile_valid = (tile_start < pends[-1]).astype(I32)
    tile_e = jnp.minimum(jnp.sum(pends[None, :] <= tile_start[:, None], axis=1), N_EXPERTS - 1).astype(I32)
    prev_e = jnp.concatenate([jnp.full((1,), -1, I32), tile_e[:-1]])
    tile_first = jnp.logical_and(tile_e != prev_e, tile_valid > 0).astype(I32)
    tile_slot = ((jnp.cumsum(tile_first) - 1) % 2).astype(I32)
    first_pos = jnp.where(tile_first > 0, jnp.arange(n_tiles, dtype=I32), n_tiles)
    next_first = jnp.concatenate([lax.cummin(first_pos[::-1])[::-1][1:], jnp.full((1,), n_tiles, I32)])
    tile_next = jnp.where(next_first < n_tiles, tile_e[jnp.minimum(next_first, n_tiles - 1)], -1).astype(I32)
    group_end = jnp.any(jnp.logical_and((tile_start + MOE_TILE)[:, None] == pends[None, :],
                                        padded[None, :] > 0), axis=1)
    zflag = jnp.logical_or(group_end, tile_valid == 0).astype(I32)

    dest = _dest_rows(sel, _pad_lanes(pstarts.astype(I32), 0, 0).reshape(1, LANES))
    dest_flat = dest[:, :TOP_K].reshape(t * TOP_K)

    x_rows = _dispatch(dest_flat, zflag, hn)
    y_rows = _experts(tile_e, tile_valid, tile_first, tile_next, tile_slot, x_rows, moe_w_gate_up,
                      moe_b_gate_up.reshape(N_EXPERTS, 1, 2 * D_EXPERT), moe_w_down,
                      moe_b_down.reshape(N_EXPERTS, 1, D_MODEL))

    return _combine(dest_flat, h, gate, row(out_norm_w), y_rows)


def kernel(x, norm_mix_w, w_in, rg_conv_w, rg_conv_b, rg_gate_a_w, rg_gate_a_b, rg_gate_x_w,
           rg_gate_x_b, rg_lambda, gdn_conv_w, gdn_A_log, gdn_dt_bias, gdn_norm_w, w_branch_rg,
           w_branch_gdn, w_out, norm_ffn_w, router_w, router_b, moe_w_gate_up, moe_b_gate_up,
           moe_w_down, moe_b_down, norm_final_w):
    assert x.shape[0] == 1 and norm_mix_w.shape[0] == 1
    return _layer(x, norm_mix_w[0], w_in[0], rg_conv_w[0], rg_conv_b[0], rg_gate_a_w[0],
                 rg_gate_a_b[0], rg_gate_x_w[0], rg_gate_x_b[0], rg_lambda[0], gdn_conv_w[0],
                 gdn_A_log[0], gdn_dt_bias[0], gdn_norm_w[0], w_branch_rg[0], w_branch_gdn[0],
                 w_out[0], norm_ffn_w[0], router_w[0], router_b[0], moe_w_gate_up[0],
                  moe_b_gate_up[0], moe_w_down[0], moe_b_down[0], norm_final_w)
```
